```python
import math
import jax, jax.numpy as jnp
from jax import lax
import numpy as np

D_MODEL = 1024
BATCH = 8
SEQ = 4096
DEPTH = 1

CHUNK = 64
Q_BLOCK = 128
EPS = 1e-6

MIX_WIDTH = D_MODEL
SB_HEAD_DIM = 64
SB_HEADS = (MIX_WIDTH // 2) // SB_HEAD_DIM
SB_WIDTH = SB_HEADS * SB_HEAD_DIM
MLA_V = 64
MLA_HEADS = (MIX_WIDTH - SB_WIDTH) // MLA_V
MLA_WIDTH = MLA_HEADS * MLA_V
MLA_NOPE = 64
MLA_ROPE = 32
MLA_Q_RANK = 384
MLA_KV_RANK = 256
ROPE_THETA = 10000.0

IN_SPLITS = (SB_WIDTH, SB_WIDTH, SB_WIDTH, MLA_Q_RANK, MLA_KV_RANK, MLA_ROPE)
IN_WIDTH = sum(IN_SPLITS)

MEM_TOKENS = 256
MEM_HEADS = 4
MEM_HEAD_DIM = D_MODEL // MEM_HEADS

N_GROUPS = 4
EXPERTS_PER_GROUP = 4
N_EXPERTS = N_GROUPS * EXPERTS_PER_GROUP
EXPERT_TOP_K = 2
EXPERT_FF = 512

kernel_name = "hybrid_stickbreak_mla_mem_hmoe_layer"


def rms_norm(x, g):
    xf = x.astype(jnp.float32)
    y = xf * lax.rsqrt(jnp.mean(xf * xf, axis=-1, keepdims=True) + EPS)
    return (y * g.astype(jnp.float32)).astype(x.dtype)


def rope_tables(positions, rot_dim, dtype):
    half = rot_dim // 2
    inv_freq = ROPE_THETA ** (-(jnp.arange(half, dtype=jnp.float32) * 2.0 / rot_dim))
    ang = positions.astype(jnp.float32)[..., None] * inv_freq
    return jnp.cos(ang).astype(dtype), jnp.sin(ang).astype(dtype)


def apply_rope(x, cos, sin):
    half = x.shape[-1] // 2
    x1, x2 = x[..., :half], x[..., half:]
    return jnp.concatenate([x1 * cos - x2 * sin, x2 * cos + x1 * sin], axis=-1)


def stick_breaking_attention(q, k, v):
    S, dh = q.shape[2], q.shape[3]
    scale = dh ** -0.5
    outs = []
    for qb in range(S // Q_BLOCK):
        q0 = qb * Q_BLOCK
        L = q0 + Q_BLOCK
        z = jnp.einsum('bhqd,bhkd->bhqk', q[:, :, q0:L], k[:, :, :L]).astype(jnp.float32) * scale
        t_idx = q0 + jnp.arange(Q_BLOCK)[:, None]
        s_idx = jnp.arange(L)[None, :]
        past = s_idx < t_idx
        log_fail = jnp.where(past, jax.nn.log_sigmoid(-z), 0.0)
        between = lax.cumsum(log_fail, axis=3, reverse=True) - log_fail
        w = jnp.where(past, jnp.exp(jax.nn.log_sigmoid(z) + between), 0.0)
        outs.append(jnp.einsum('bhqk,bhkd->bhqd', w.astype(v.dtype), v[:, :, :L]))
    return jnp.concatenate(outs, axis=2)


def chunk_causal_attention(q, k, v):
    S, dk = q.shape[2], q.shape[3]
    scale = dk ** -0.5
    outs = []
    for qb in range(S // Q_BLOCK):
        q0 = qb * Q_BLOCK
        L = q0 + Q_BLOCK
        s = jnp.einsum('bhqd,bhkd->bhqk', q[:, :, q0:L], k[:, :, :L]).astype(jnp.float32) * scale
        t_chunk = (q0 + jnp.arange(Q_BLOCK))[:, None] // CHUNK
        s_chunk = jnp.arange(L)[None, :] // CHUNK
        s = jnp.where(s_chunk <= t_chunk, s, -jnp.inf)
        p = jax.nn.softmax(s, axis=-1)
        outs.append(jnp.einsum('bhqk,bhkd->bhqd', p.astype(v.dtype), v[:, :, :L]))
    return jnp.concatenate(outs, axis=2)


def setup_inputs(seed: int = 0) -> dict:
    key = jax.random.key(seed)
    ks = jax.random.split(key, 40)
    f32 = jnp.float32

    def w(i, shape, fan_in):
        return jax.random.normal(ks[i], shape, f32) * (fan_in ** -0.5)

    def gain(i, n):
        return 1.0 + 0.05 * jax.random.normal(ks[i], (n,), f32)

    x = jax.random.normal(ks[0], (BATCH, SEQ, D_MODEL), f32)
    mem = jax.random.normal(ks[1], (BATCH, MEM_TOKENS, D_MODEL), f32)
    offsets = jax.random.randint(ks[2], (BATCH, 1), 0, 4096, dtype=jnp.int32)
    positions = offsets + jnp.arange(SEQ, dtype=jnp.int32)[None, :]
    return {
        "x": x,
        "mem": mem,
        "positions": positions,
        "norm_mix_g": gain(3, D_MODEL),
        "w_in": w(4, (D_MODEL, IN_WIDTH), D_MODEL),
        "mla_q_norm_g": gain(5, MLA_Q_RANK),
        "w_uq": w(6, (MLA_Q_RANK, MLA_HEADS * (MLA_NOPE + MLA_ROPE)), MLA_Q_RANK),
        "mla_kv_norm_g": gain(7, MLA_KV_RANK),
        "w_ukv": w(8, (MLA_KV_RANK, MLA_HEADS * (MLA_NOPE + MLA_V)), MLA_KV_RANK),
        "mla_qn_g": gain(9, MLA_NOPE),
        "mla_qr_g": gain(10, MLA_ROPE),
        "mla_kn_g": gain(11, MLA_NOPE),
        "mla_kr_g": gain(12, MLA_ROPE),
        "sb_out_g": gain(13, SB_WIDTH),
        "mla_out_g": gain(14, MLA_WIDTH),
        "w_o": w(15, (MIX_WIDTH, D_MODEL), MIX_WIDTH),
        "norm_mem_g": gain(16, D_MODEL),
        "mem_src_g": gain(17, D_MODEL),
        "w_mq": w(18, (D_MODEL, D_MODEL), D_MODEL),
        "w_mkv": w(19, (D_MODEL, 2 * D_MODEL), D_MODEL),
        "mem_qhead_g": gain(20, MEM_HEAD_DIM),
        "mem_khead_g": gain(21, MEM_HEAD_DIM),
        "w_mo": w(22, (D_MODEL, D_MODEL), D_MODEL),
        "norm_moe_g": gain(23, D_MODEL),
        "w_group": w(24, (D_MODEL, N_GROUPS), D_MODEL),
        "b_group": 0.01 * jax.random.normal(ks[25], (N_GROUPS,), f32),
        "w_router": w(26, (D_MODEL, N_EXPERTS), D_MODEL),
        "b_router": 0.01 * jax.random.normal(ks[27], (N_EXPERTS,), f32),
        "w_gate": w(28, (N_EXPERTS, D_MODEL, EXPERT_FF), D_MODEL),
        "w_up": w(29, (N_EXPERTS, D_MODEL, EXPERT_FF), D_MODEL),
        "w_down": w(30, (N_EXPERTS, EXPERT_FF, D_MODEL), EXPERT_FF),
    }


def reference(x, mem, positions, norm_mix_g, w_in, mla_q_norm_g, w_uq, mla_kv_norm_g, w_ukv,
              mla_qn_g, mla_qr_g, mla_kn_g, mla_kr_g, sb_out_g, mla_out_g, w_o,
              norm_mem_g, mem_src_g, w_mq, w_mkv, mem_qhead_g, mem_khead_g, w_mo,
              norm_moe_g, w_group, b_group, w_router, b_router, w_gate, w_up, w_down):
    B, S, D = x.shape
    cos, sin = rope_tables(positions, MLA_ROPE, x.dtype)
    h = x
    for _ in range(DEPTH):
        hn = rms_norm(h, norm_mix_g)
        proj = hn @ w_in
        cuts = list(np.cumsum(IN_SPLITS)[:-1])
        sb_q, sb_k, sb_v, c_q, c_kv, k_rope = jnp.split(proj, cuts, axis=-1)

        to_heads = lambda t: t.reshape(B, S, SB_HEADS, SB_HEAD_DIM).transpose(0, 2, 1, 3)
        sb = stick_breaking_attention(to_heads(sb_q), to_heads(sb_k), to_heads(sb_v))
        sb = sb.transpose(0, 2, 1, 3).reshape(B, S, SB_WIDTH)

        q = (rms_norm(c_q, mla_q_norm_g) @ w_uq).reshape(B, S, MLA_HEADS, MLA_NOPE + MLA_ROPE)
        kv = (rms_norm(c_kv, mla_kv_norm_g) @ w_ukv).reshape(B, S, MLA_HEADS, MLA_NOPE + MLA_V)
        q_nope = rms_norm(q[..., :MLA_NOPE], mla_qn_g)
        q_rope = rms_norm(q[..., MLA_NOPE:], mla_qr_g)
        k_nope = rms_norm(kv[..., :MLA_NOPE], mla_kn_g)
        v_mla = kv[..., MLA_NOPE:]
        k_rope = rms_norm(k_rope, mla_kr_g)
        q_rope = apply_rope(q_rope, cos[:, :, None], sin[:, :, None])
        k_rope = apply_rope(k_rope, cos, sin)
        q_full = jnp.concatenate([q_nope, q_rope], axis=-1)
        k_full = jnp.concatenate(
            [k_nope, jnp.broadcast_to(k_rope[:, :, None], (B, S, MLA_HEADS, MLA_ROPE))], axis=-1)
        mla = chunk_causal_attention(q_full.transpose(0, 2, 1, 3), k_full.transpose(0, 2, 1, 3),
                                     v_mla.transpose(0, 2, 1, 3))
        mla = mla.transpose(0, 2, 1, 3).reshape(B, S, MLA_WIDTH)

        mixed = jnp.concatenate([rms_norm(sb, sb_out_g), rms_norm(mla, mla_out_g)], axis=-1)
        h = h + mixed @ w_o

        hq = rms_norm(h, norm_mem_g)
        mn = rms_norm(mem, mem_src_g)
        mq = rms_norm((hq @ w_mq).reshape(B, S, MEM_HEADS, MEM_HEAD_DIM), mem_qhead_g)
        mkv = (mn @ w_mkv).reshape(B, MEM_TOKENS, 2, MEM_HEADS, MEM_HEAD_DIM)
        mk = rms_norm(mkv[:, :, 0], mem_khead_g)
        mv = mkv[:, :, 1]
        sc = jnp.einsum('bshd,bmhd->bhsm', mq, mk).astype(jnp.float32) * (MEM_HEAD_DIM ** -0.5)
        p = jax.nn.softmax(sc, axis=-1).astype(mv.dtype)
        mo = jnp.einsum('bhsm,bmhd->bshd', p, mv).reshape(B, S, D)
        h = h + mo @ w_mo

        t = rms_norm(h, norm_moe_g).reshape(B * S, D)
        g_prob = jax.nn.softmax((t @ w_group).astype(jnp.float32) + b_group.astype(jnp.float32), axis=-1)
        g_idx = jnp.argmax(g_prob, axis=-1)
        g_w = jnp.take_along_axis(g_prob, g_idx[:, None], axis=-1)
        e_logits = ((t @ w_router).astype(jnp.float32) + b_router.astype(jnp.float32)
                    ).reshape(-1, N_GROUPS, EXPERTS_PER_GROUP)
        e_logits = jnp.take_along_axis(e_logits, g_idx[:, None, None], axis=1)[:, 0]
        e_prob = jax.nn.softmax(e_logits, axis=-1)
        top_p, top_i = lax.top_k(e_prob, EXPERT_TOP_K)
        weights = g_w * top_p / jnp.sum(top_p, axis=-1, keepdims=True)
        expert_id = g_idx[:, None] * EXPERTS_PER_GROUP + top_i
        gates = jnp.sum(jax.nn.one_hot(expert_id, N_EXPERTS, dtype=jnp.float32) * weights[..., None],
                        axis=1).astype(t.dtype)
        moe = jnp.zeros_like(t)
        for e in range(N_EXPERTS):
            he = jax.nn.silu(t @ w_gate[e]) * (t @ w_up[e])
            moe = moe + gates[:, e:e + 1] * (he @ w_down[e])
        h = h + moe.reshape(B, S, D)
    return h
```

```python
import functools

import numpy as np
import jax
import jax.numpy as jnp
from jax import lax
from jax.experimental import pallas as pl
from jax.experimental.pallas import tpu as pltpu

F32 = jnp.float32
BF16 = jnp.bfloat16
EPS = 1e-6
LANES = 128

CHUNK = 64
SB_HEADS = 8
SB_HEAD_DIM = 64
SB_WIDTH = SB_HEADS * SB_HEAD_DIM
MLA_HEADS = 8
MLA_NOPE = 64
MLA_ROPE = 32
MLA_V = 64
MLA_WIDTH = MLA_HEADS * MLA_V
MLA_Q_RANK = 384
MLA_KV_RANK = 256
ROPE_THETA = 10000.0
MEM_HEADS = 4
N_GROUPS = 4
EXPERTS_PER_GROUP = 4
N_EXPERTS = N_GROUPS * EXPERTS_PER_GROUP
N_PAIRS = MLA_HEADS // 2
PAIR_W = 2 * LANES
SB_EXP_UNDERFLOW = 104.0

_TRANS_B = (((1,), (1,)), ((), ()))


def _rms(x, g):
    return x * lax.rsqrt(jnp.mean(x * x, axis=-1, keepdims=True) + EPS) * g


def _seg_rms(x, masks, seg_len, g):
    x2 = x * x
    r = jnp.zeros_like(x)
    for m in masks:
        s = jnp.sum(jnp.where(m, x2, 0.0), axis=-1, keepdims=True)
        r = jnp.where(m, lax.rsqrt(s * (1.0 / seg_len) + EPS), r)
    return x * r * g


def _mem_kv_kernel(mem_ref, gsrc_ref, w_ref, gk_ref, mk_ref, mv_ref):
    d = mem_ref.shape[-1]
    hd = d // MEM_HEADS
    mn = _rms(mem_ref[0], gsrc_ref[...]).astype(BF16)
    kv = jnp.dot(mn, w_ref[...], preferred_element_type=F32)
    for h in range(MEM_HEADS):
        kh = kv[:, h * hd:(h + 1) * hd]
        mk_ref[0, :, h * hd:(h + 1) * hd] = _rms(kh, gk_ref[...]).astype(BF16)
    mv_ref[0] = kv[:, d:].astype(BF16)


def _in_proj_kernel(x_ref, pos_ref, gmix_ref, w1_ref, gq_ref, wuq_ref, gkv_ref, wukv_ref,
                    gqcat_ref, gkcat_ref, gkr_ref, invf_ref, sgn_ref,
                    sbq_ref, sbk_ref, sbv_ref, mq_ref, mk_ref, mv_ref):
    hn = _rms(x_ref[...], gmix_ref[...]).astype(BF16)
    proj = jnp.dot(hn, w1_ref[...], preferred_element_type=F32)
    sbq_ref[...] = proj[:, 0:SB_WIDTH].astype(BF16)
    sbk_ref[...] = proj[:, SB_WIDTH:2 * SB_WIDTH].astype(BF16)
    sbv_ref[...] = proj[:, 2 * SB_WIDTH:3 * SB_WIDTH].astype(BF16)
    o = 3 * SB_WIDTH
    cq = proj[:, o:o + MLA_Q_RANK]
    ckv = proj[:, o + MLA_Q_RANK:o + MLA_Q_RANK + MLA_KV_RANK]
    kr = proj[:, o + MLA_Q_RANK + MLA_KV_RANK:]

    q = jnp.dot(_rms(cq, gq_ref[...]).astype(BF16), wuq_ref[...], preferred_element_type=F32)
    kv = jnp.dot(_rms(ckv, gkv_ref[...]).astype(BF16), wukv_ref[...], preferred_element_type=F32)

    lane = lax.broadcasted_iota(jnp.int32, (1, LANES), 1)
    lr = lane & 63
    lo, hi = lane < 64, lane >= 64
    r0, r1 = lr < 16, (lr >= 16) & (lr < 32)

    ang = pos_ref[...].astype(F32) * invf_ref[...]
    cos_t = jnp.cos(ang)
    sin_t = jnp.sin(ang) * sgn_ref[...]

    def rope(b):
        return b * cos_t + pltpu.roll(b, 64, 1) * sin_t

    q_scale = (MLA_NOPE + MLA_ROPE) ** -0.5
    krn = rope(_seg_rms(kr, (r0, r1), MLA_ROPE, gkr_ref[...]))
    for p in range(N_PAIRS):
        b0 = p * PAIR_W
        qa = _seg_rms(q[:, b0:b0 + LANES], (lo, hi), MLA_NOPE, gqcat_ref[:, b0:b0 + LANES])
        qb = rope(_seg_rms(q[:, b0 + LANES:b0 + PAIR_W], (r0, r1), MLA_ROPE,
                           gqcat_ref[:, b0 + LANES:b0 + PAIR_W]))
        mq_ref[:, b0:b0 + LANES] = (qa * q_scale).astype(BF16)
        mq_ref[:, b0 + LANES:b0 + PAIR_W] = (qb * q_scale).astype(BF16)
        ka = _seg_rms(kv[:, b0:b0 + LANES], (lo, hi), MLA_NOPE, gkcat_ref[:, b0:b0 + LANES])
        mk_ref[:, b0:b0 + LANES] = ka.astype(BF16)
        mk_ref[:, b0 + LANES:b0 + PAIR_W] = krn.astype(BF16)
    mv_ref[...] = kv[:, N_PAIRS * PAIR_W:].astype(BF16)


def _sb_attn_kernel(q_ref, k_ref, v_ref, o_ref, *, tq):
    i = pl.program_id(2)
    q = q_ref[0]
    lane = lax.broadcasted_iota(jnp.int32, (1, LANES), 1)
    row = lax.broadcasted_iota(jnp.int32, (tq, tq), 0)
    col = lax.broadcasted_iota(jnp.int32, (tq, tq), 1)
    past = col < row
    later = (row > col).astype(BF16)

    def tile(qm, j, carry, diag):
        start = pl.multiple_of(j * tq, tq)
        ks = k_ref[0, pl.ds(start, tq), :]
        vs = v_ref[0, pl.ds(start, tq), :]
        z = lax.dot_general(qm, ks, _TRANS_B, preferred_element_type=F32)
        sp = jnp.maximum(z, 0.0) + jnp.log1p(jnp.exp(-jnp.abs(z)))
        lf = -sp
        if diag:
            lf = jnp.where(past, lf, 0.0)
        between = carry + jnp.dot(lf.astype(BF16), later, preferred_element_type=F32)
        w = jnp.exp((z - sp) + between)
        if diag:
            w = jnp.where(past, w, 0.0)
        pv = jnp.dot(w.astype(BF16), vs, preferred_element_type=F32)
        return pv, carry + jnp.sum(lf, axis=1, keepdims=True)

    outs = []
    for h in range(2):
        hm = (lane >= 64 * h) & (lane < 64 * (h + 1))
        qm = jnp.where(hm, q, jnp.zeros_like(q))
        pv0, carry0 = tile(qm, i, jnp.zeros((tq, 1), F32), True)

        def cond(st):
            j, m, _, _ = st
            return (j >= 0) & (m > -SB_EXP_UNDERFLOW)

        def body(st, qm=qm):
            j, _, carry, acc = st
            pv, carry = tile(qm, j, carry, False)
            return j - 1, jnp.max(carry), carry, acc + pv

        _, _, _, acc = lax.while_loop(cond, body, (i - 1, jnp.max(carry0), carry0, pv0))
        outs.append(acc)
    o_ref[0] = jnp.where(lane < 64, outs[0], outs[1]).astype(o_ref.dtype)


def _mla_attn_kernel(q_ref, k_ref, v_ref, o_ref, *, tq):
    i = pl.program_id(2)
    q = q_ref[0]
    lane = lax.broadcasted_iota(jnp.int32, (1, LANES), 1)
    l2 = lax.broadcasted_iota(jnp.int32, (1, PAIR_W), 1)
    lr = l2 & 63
    row = lax.broadcasted_iota(jnp.int32, (tq, tq), 0)
    col = lax.broadcasted_iota(jnp.int32, (tq, tq), 1)
    visible = (col // CHUNK) <= (row // CHUNK)

    def step(qm, j, m, l, acc, diag):
        start = pl.multiple_of(j * tq, tq)
        ks = k_ref[0, pl.ds(start, tq), :]
        vs = v_ref[0, pl.ds(start, tq), :]
        s = lax.dot_general(qm, ks, _TRANS_B, preferred_element_type=F32)
        if diag:
            s = jnp.where(visible, s, -jnp.inf)
        m_new = jnp.maximum(m, jnp.max(s, axis=1, keepdims=True))
        alpha = jnp.exp(m - m_new)
        p = jnp.exp(s - m_new)
        l_new = alpha * l + jnp.sum(p, axis=1, keepdims=True)
        acc_new = alpha * acc + jnp.dot(p.astype(BF16), vs, preferred_element_type=F32)
        return m_new, l_new, acc_new

    outs = []
    for h in range(2):
        if h == 0:
            hm = (l2 < 64) | ((l2 >= LANES) & (lr < 16))
        else:
            hm = ((l2 >= 64) & (l2 < LANES)) | ((l2 >= LANES) & (lr >= 16) & (lr < 32))
        qm = jnp.where(hm, q, jnp.zeros_like(q))
        st = step(qm, i, jnp.full((tq, 1), -jnp.inf, F32), jnp.zeros((tq, 1), F32),
                  jnp.zeros((tq, LANES), F32), True)
        m, l, acc = lax.fori_loop(0, i, lambda j, s, qm=qm: step(qm, j, *s, False), st)
        outs.append(acc / l)
    o_ref[0] = jnp.where(lane < 64, outs[0], outs[1]).astype(o_ref.dtype)


def _post_attn_kernel(x_ref, sb_ref, mla_ref, gsb_ref, gmla_ref, wo_ref, gmem_ref, wmq_ref, gqh_ref,
                      mk_ref, mv_ref, wmo_ref, gmoe_ref, wr_hi_ref, wr_lo_ref, br_ref,
                      h2_ref, t_ref, gates_ref):
    d = x_ref.shape[-1]
    hd = d // MEM_HEADS
    sbn = _rms(sb_ref[...].astype(F32), gsb_ref[...]).astype(BF16)
    mlan = _rms(mla_ref[...].astype(F32), gmla_ref[...]).astype(BF16)
    mixed = jnp.concatenate([sbn, mlan], axis=1)
    h1 = x_ref[...] + jnp.dot(mixed, wo_ref[...], preferred_element_type=F32)

    hq = _rms(h1, gmem_ref[...]).astype(BF16)
    mq = jnp.dot(hq, wmq_ref[...], preferred_element_type=F32)
    mos = []
    for h in range(MEM_HEADS):
        qh = (_rms(mq[:, h * hd:(h + 1) * hd], gqh_ref[...]) * (hd ** -0.5)).astype(BF16)
        sc = lax.dot_general(qh, mk_ref[0, :, h * hd:(h + 1) * hd], _TRANS_B, preferred_element_type=F32)
        sc = sc - jnp.max(sc, axis=-1, keepdims=True)
        e = jnp.exp(sc)
        p = (e / jnp.sum(e, axis=-1, keepdims=True)).astype(BF16)
        mos.append(jnp.dot(p, mv_ref[0, :, h * hd:(h + 1) * hd], preferred_element_type=F32).astype(BF16))
    h2 = h1 + jnp.dot(jnp.concatenate(mos, axis=1), wmo_ref[...], preferred_element_type=F32)
    h2_ref[...] = h2

    t = _rms(h2, gmoe_ref[...])
    t_hi = t.astype(BF16)
    t_ref[...] = t_hi
    t_lo = (t - t_hi.astype(F32)).astype(BF16)
    logits = (jnp.dot(t_hi, wr_hi_ref[...], preferred_element_type=F32)
              + jnp.dot(t_hi, wr_lo_ref[...], preferred_element_type=F32)
              + jnp.dot(t_lo, wr_hi_ref[...], preferred_element_type=F32)) + br_ref[...]

    lane = lax.broadcasted_iota(jnp.int32, (1, LANES), 1).astype(F32)
    big = float(LANES)
    neg = -jnp.inf
    lg = jnp.where(lane < N_GROUPS, logits, neg)
    gmax = jnp.max(lg, axis=-1, keepdims=True)
    g_idx = jnp.min(jnp.where(lg == gmax, lane, big), axis=-1, keepdims=True)
    g_w = 1.0 / jnp.sum(jnp.exp(lg - gmax), axis=-1, keepdims=True)
    e_lo = N_GROUPS + g_idx * EXPERTS_PER_GROUP
    in_grp = (lane >= e_lo) & (lane < e_lo + EXPERTS_PER_GROUP)
    le = jnp.where(in_grp, logits, neg)
    m1 = jnp.max(le, axis=-1, keepdims=True)
    i1 = jnp.min(jnp.where(le == m1, lane, big), axis=-1, keepdims=True)
    le2 = jnp.where(lane == i1, neg, le)
    m2 = jnp.max(le2, axis=-1, keepdims=True)
    i2 = jnp.min(jnp.where(le2 == m2, lane, big), axis=-1, keepdims=True)
    e2 = jnp.exp(m2 - m1)
    w1 = g_w / (1.0 + e2)
    w2 = g_w * e2 / (1.0 + e2)
    gates_ref[...] = jnp.where(lane == i1, w1, 0.0) + jnp.where(lane == i2, w2, 0.0)


def _moe_kernel(t_ref, gates_ref, h2_ref, wg_ref, wu_ref, wd_ref, o_ref):
    e = pl.program_id(1)

    @pl.when(e == 0)
    def _():
        o_ref[...] = h2_ref[...]

    lane = lax.broadcasted_iota(jnp.int32, (1, LANES), 1)
    ge = jnp.sum(jnp.where(lane == N_GROUPS + e, gates_ref[...], 0.0), axis=-1, keepdims=True)
    t = t_ref[...]
    a = jnp.dot(t, wg_ref[0], preferred_element_type=F32)
    u = jnp.dot(t, wu_ref[0], preferred_element_type=F32)
    he = (a * jax.nn.sigmoid(a) * u).astype(BF16)
    o_ref[...] += ge * jnp.dot(he, wd_ref[0], preferred_element_type=F32)


def _rope_block(x1, x2):
    z = jnp.zeros(x1.shape[:-1] + (32,), x1.dtype)
    return jnp.concatenate([x1, x1, z, x2, x2, z], axis=-1)


def _pair_rope_block(a, b):
    z = jnp.zeros(a.shape[:-1] + (32,), a.dtype)
    return jnp.concatenate([a[..., :16], b[..., :16], z, a[..., 16:], b[..., 16:], z], axis=-1)


def _row(v):
    return v.reshape(1, -1).astype(F32)


def kernel(x, mem, positions, norm_mix_g, w_in, mla_q_norm_g, w_uq, mla_kv_norm_g, w_ukv, mla_qn_g, mla_qr_g, mla_kn_g, mla_kr_g, sb_out_g, mla_out_g, w_o, norm_mem_g, mem_src_g, w_mq, w_mkv, mem_qhead_g, mem_khead_g, w_mo, norm_moe_g, w_group, b_group, w_router, b_router, w_gate, w_up, w_down):
    B, S, D = x.shape
    T = B * S
    M = mem.shape[1]
    FF = w_gate.shape[-1]
    half = MLA_ROPE // 2
    qk = MLA_NOPE + MLA_ROPE

    o = 3 * SB_WIDTH + MLA_Q_RANK + MLA_KV_RANK
    w1 = jnp.concatenate([w_in[:, :SB_WIDTH] * (SB_HEAD_DIM ** -0.5), w_in[:, SB_WIDTH:o],
                          _rope_block(w_in[:, o:o + half], w_in[:, o + half:])], axis=1).astype(BF16)
    uq = w_uq.reshape(MLA_Q_RANK, MLA_HEADS, qk)
    ukv = w_ukv.reshape(MLA_KV_RANK, MLA_HEADS, MLA_NOPE + MLA_V)
    zk = jnp.zeros((MLA_KV_RANK, LANES), F32)
    uq_cols, uk_cols = [], []
    for p in range(N_PAIRS):
        a, b = 2 * p, 2 * p + 1
        uq_cols += [uq[:, a, :MLA_NOPE], uq[:, b, :MLA_NOPE], _pair_rope_block(uq[:, a, MLA_NOPE:], uq[:, b, MLA_NOPE:])]
        uk_cols += [ukv[:, a, :MLA_NOPE], ukv[:, b, :MLA_NOPE], zk]
    wuq = jnp.concatenate(uq_cols, axis=1).astype(BF16)
    wukv = jnp.concatenate(uk_cols + [ukv[:, h, MLA_NOPE:] for h in range(MLA_HEADS)], axis=1).astype(BF16)
    gq_pair = jnp.concatenate([mla_qn_g, mla_qn_g, _pair_rope_block(mla_qr_g, mla_qr_g)])
    gk_pair = jnp.concatenate([mla_kn_g, mla_kn_g, jnp.zeros((LANES,), F32)])
    gqcat = _row(jnp.tile(gq_pair, N_PAIRS))
    gkcat = _row(jnp.tile(gk_pair, N_PAIRS))
    gkr = _row(_rope_block(mla_kr_g[:half], mla_kr_g[half:]))
    inv_freq = ROPE_THETA ** (-(jnp.arange(half, dtype=F32) * 2.0 / MLA_ROPE))
    invf = _row(_rope_block(inv_freq, inv_freq))
    sgn = _row(jnp.concatenate([-jnp.ones((64,), F32), jnp.ones((64,), F32)]))
    wr = jnp.concatenate([w_group, w_router, jnp.zeros((D, LANES - N_GROUPS - N_EXPERTS), F32)], axis=1)
    wr_hi = wr.astype(BF16)
    wr_lo = (wr - wr_hi.astype(F32)).astype(BF16)
    br = _row(jnp.concatenate([b_group, b_router, jnp.zeros((LANES - N_GROUPS - N_EXPERTS,), F32)]))

    full = lambda shape: pl.BlockSpec(shape, lambda *_: (0,) * len(shape))

    mk, mv = pl.pallas_call(
        _mem_kv_kernel, grid=(B,), name="mem_kv",
        in_specs=[pl.BlockSpec((1, M, D), lambda b: (b, 0, 0)), full((1, D)), full((D, 2 * D)),
                  full((1, D // MEM_HEADS))],
        out_specs=[pl.BlockSpec((1, M, D), lambda b: (b, 0, 0))] * 2,
        out_shape=[jax.ShapeDtypeStruct((B, M, D), BF16)] * 2,
    )(mem, _row(mem_src_g), w_mkv.astype(BF16), _row(mem_khead_g))

    tm = 256
    rows = lambda w: pl.BlockSpec((tm, w), lambda i: (i, 0))
    sbq, sbk, sbv, mq, mkk, mvv = pl.pallas_call(
        _in_proj_kernel, grid=(T // tm,), name="in_proj",
        in_specs=[rows(D), rows(1), full((1, D)), full(w1.shape), full((1, MLA_Q_RANK)), full(wuq.shape),
                  full((1, MLA_KV_RANK)), full(wukv.shape), full(gqcat.shape), full(gkcat.shape),
                  full((1, LANES)), full((1, LANES)), full((1, LANES))],
        out_specs=[rows(SB_WIDTH)] * 3 + [rows(N_PAIRS * PAIR_W)] * 2 + [rows(MLA_WIDTH)],
        out_shape=[jax.ShapeDtypeStruct((T, SB_WIDTH), BF16)] * 3
                  + [jax.ShapeDtypeStruct((T, N_PAIRS * PAIR_W), BF16)] * 2
                  + [jax.ShapeDtypeStruct((T, MLA_WIDTH), BF16)],
    )(x.reshape(T, D), positions.reshape(T, 1), _row(norm_mix_g), w1, _row(mla_q_norm_g), wuq,
      _row(mla_kv_norm_g), wukv, gqcat, gkcat, gkr, invf, sgn)

    tq = 256
    qspec = lambda w: pl.BlockSpec((1, tq, w), lambda b, p, i: (b, i, p))
    kvspec = lambda w: pl.BlockSpec((1, S, w), lambda b, p, i: (b, 0, p))
    sb = pl.pallas_call(
        functools.partial(_sb_attn_kernel, tq=tq), grid=(B, SB_HEADS // 2, S // tq), name="sb_attn",
        in_specs=[qspec(LANES), kvspec(LANES), kvspec(LANES)],
        out_specs=qspec(LANES),
        out_shape=jax.ShapeDtypeStruct((B, S, SB_WIDTH), BF16),
    )(sbq.reshape(B, S, SB_WIDTH), sbk.reshape(B, S, SB_WIDTH), sbv.reshape(B, S, SB_WIDTH))
    mla = pl.pallas_call(
        functools.partial(_mla_attn_kernel, tq=tq), grid=(B, N_PAIRS, S // tq), name="mla_attn",
        in_specs=[qspec(PAIR_W), kvspec(PAIR_W), kvspec(LANES)],
        out_specs=qspec(LANES),
        out_shape=jax.ShapeDtypeStruct((B, S, MLA_WIDTH), BF16),
    )(mq.reshape(B, S, -1), mkk.reshape(B, S, -1), mvv.reshape(B, S, MLA_WIDTH))

    tp = 256
    prow = lambda w: pl.BlockSpec((tp, w), lambda i: (i, 0))
    memspec = pl.BlockSpec((1, M, D), lambda i: ((i * tp) // S, 0, 0))
    h2, t_bf, gates = pl.pallas_call(
        _post_attn_kernel, grid=(T // tp,), name="post_attn",
        in_specs=[prow(D), prow(SB_WIDTH), prow(MLA_WIDTH), full((1, SB_WIDTH)), full((1, MLA_WIDTH)),
                  full((D, D)), full((1, D)), full((D, D)), full((1, D // MEM_HEADS)), memspec, memspec,
                  full((D, D)), full((1, D)), full((D, LANES)), full((D, LANES)), full((1, LANES))],
        out_specs=[prow(D), prow(D), prow(LANES)],
        out_shape=[jax.ShapeDtypeStruct((T, D), F32), jax.ShapeDtypeStruct((T, D), BF16),
                   jax.ShapeDtypeStruct((T, LANES), F32)],
    )(x.reshape(T, D), sb.reshape(T, SB_WIDTH), mla.reshape(T, MLA_WIDTH), _row(sb_out_g), _row(mla_out_g),
      w_o.astype(BF16), _row(norm_mem_g), w_mq.astype(BF16), _row(mem_qhead_g), mk, mv,
      w_mo.astype(BF16), _row(norm_moe_g), wr_hi, wr_lo, br)

    te = 1024
    erow = lambda w: pl.BlockSpec((te, w), lambda i, e: (i, 0))
    out = pl.pallas_call(
        _moe_kernel, grid=(T // te, N_EXPERTS), name="moe",
        in_specs=[erow(D), erow(LANES), erow(D),
                  pl.BlockSpec((1, D, FF), lambda i, e: (e, 0, 0)),
                  pl.BlockSpec((1, D, FF), lambda i, e: (e, 0, 0)),
                  pl.BlockSpec((1, FF, D), lambda i, e: (e, 0, 0))],
        out_specs=erow(D),
        out_shape=jax.ShapeDtypeStruct((T, D), F32),
    )(t_bf, gates, h2, w_gate.astype(BF16), w_up.astype(BF16), w_down.astype(BF16))
    return out.reshape(B, S, D)
```

```python
import functools

import numpy as np
import jax
import jax.numpy as jnp
from jax import lax
from jax.experimental import pallas as pl
from jax.experimental.pallas import tpu as pltpu

F32 = jnp.float32
BF16 = jnp.bfloat16
EPS = 1e-6
LANES = 128

CHUNK = 64
SB_HEADS = 8
SB_HEAD_DIM = 64
SB_WIDTH = SB_HEADS * SB_HEAD_DIM
MLA_HEADS = 8
MLA_NOPE = 64
MLA_ROPE = 32
MLA_V = 64
MLA_WIDTH = MLA_HEADS * MLA_V
MLA_Q_RANK = 384
MLA_KV_RANK = 256
ROPE_THETA = 10000.0
MEM_HEADS = 4
N_GROUPS = 4
EXPERTS_PER_GROUP = 4
N_EXPERTS = N_GROUPS * EXPERTS_PER_GROUP
N_PAIRS = MLA_HEADS // 2
PAIR_W = 2 * LANES
SB_EXP_UNDERFLOW = 104.0

_TRANS_B = (((1,), (1,)), ((), ()))


def _rms(x, g):
    return x * lax.rsqrt(jnp.mean(x * x, axis=-1, keepdims=True) + EPS) * g


def _seg_rms(x, masks, seg_len, g):
    x2 = x * x
    r = jnp.zeros_like(x)
    for m in masks:
        s = jnp.sum(jnp.where(m, x2, 0.0), axis=-1, keepdims=True)
        r = jnp.where(m, lax.rsqrt(s * (1.0 / seg_len) + EPS), r)
    return x * r * g


def _mem_kv_kernel(mem_ref, gsrc_ref, w_ref, gk_ref, mk_ref, mv_ref):
    d = mem_ref.shape[-1]
    hd = d // MEM_HEADS
    mn = _rms(mem_ref[0], gsrc_ref[...]).astype(BF16)
    kv = jnp.dot(mn, w_ref[...], preferred_element_type=F32)
    for h in range(MEM_HEADS):
        kh = kv[:, h * hd:(h + 1) * hd]
        mk_ref[0, :, h * hd:(h + 1) * hd] = _rms(kh, gk_ref[...]).astype(BF16)
    mv_ref[0] = kv[:, d:].astype(BF16)


def _in_proj_kernel(x_ref, pos_ref, gmix_ref, w1_ref, gq_ref, wuq_ref, gkv_ref, wukv_ref,
                    gqcat_ref, gkcat_ref, gkr_ref, invf_ref, sgn_ref,
                    sbq_ref, sbk_ref, sbv_ref, mq_ref, mk_ref, mv_ref):
    hn = _rms(x_ref[...], gmix_ref[...]).astype(BF16)
    proj = jnp.dot(hn, w1_ref[...], preferred_element_type=F32)
    sbq_ref[...] = proj[:, 0:SB_WIDTH].astype(BF16)
    sbk_ref[...] = proj[:, SB_WIDTH:2 * SB_WIDTH].astype(BF16)
    sbv_ref[...] = proj[:, 2 * SB_WIDTH:3 * SB_WIDTH].astype(BF16)
    o = 3 * SB_WIDTH
    cq = proj[:, o:o + MLA_Q_RANK]
    ckv = proj[:, o + MLA_Q_RANK:o + MLA_Q_RANK + MLA_KV_RANK]
    kr = proj[:, o + MLA_Q_RANK + MLA_KV_RANK:]

    q = jnp.dot(_rms(cq, gq_ref[...]).astype(BF16), wuq_ref[...], preferred_element_type=F32)
    kv = jnp.dot(_rms(ckv, gkv_ref[...]).astype(BF16), wukv_ref[...], preferred_element_type=F32)

    lane = lax.broadcasted_iota(jnp.int32, (1, LANES), 1)
    lr = lane & 63
    lo, hi = lane < 64, lane >= 64
    r0, r1 = lr < 16, (lr >= 16) & (lr < 32)

    ang = pos_ref[...].astype(F32) * invf_ref[...]
    cos_t = jnp.cos(ang)
    sin_t = jnp.sin(ang) * sgn_ref[...]

    def rope(b):
        return b * cos_t + pltpu.roll(b, 64, 1) * sin_t

    q_scale = (MLA_NOPE + MLA_ROPE) ** -0.5
    krn = rope(_seg_rms(kr, (r0, r1), MLA_ROPE, gkr_ref[...]))
    for p in range(N_PAIRS):
        b0 = p * PAIR_W
        qa = _seg_rms(q[:, b0:b0 + LANES], (lo, hi), MLA_NOPE, gqcat_ref[:, b0:b0 + LANES])
        qb = rope(_seg_rms(q[:, b0 + LANES:b0 + PAIR_W], (r0, r1), MLA_ROPE,
                           gqcat_ref[:, b0 + LANES:b0 + PAIR_W]))
        mq_ref[:, b0:b0 + LANES] = (qa * q_scale).astype(BF16)
        mq_ref[:, b0 + LANES:b0 + PAIR_W] = (qb * q_scale).astype(BF16)
        ka = _seg_rms(kv[:, b0:b0 + LANES], (lo, hi), MLA_NOPE, gkcat_ref[:, b0:b0 + LANES])
        mk_ref[:, b0:b0 + LANES] = ka.astype(BF16)
        mk_ref[:, b0 + LANES:b0 + PAIR_W] = krn.astype(BF16)
    mv_ref[...] = kv[:, N_PAIRS * PAIR_W:].astype(BF16)


def _sb_attn_kernel(q_ref, k_ref, v_ref, o_ref, *, tq):
    i = pl.program_id(2)
    q = q_ref[0]
    lane = lax.broadcasted_iota(jnp.int32, (1, LANES), 1)
    row = lax.broadcasted_iota(jnp.int32, (tq, tq), 0)
    col = lax.broadcasted_iota(jnp.int32, (tq, tq), 1)
    past = col < row
    later = (row > col).astype(BF16)

    def tile(qm, j, carry, diag):
        start = pl.multiple_of(j * tq, tq)
        ks = k_ref[0, pl.ds(start, tq), :]
        vs = v_ref[0, pl.ds(start, tq), :]
        z = lax.dot_general(qm, ks, _TRANS_B, preferred_element_type=F32)
        sp = jnp.maximum(z, 0.0) + jnp.log1p(jnp.exp(-jnp.abs(z)))
        lf = -sp
        if diag:
            lf = jnp.where(past, lf, 0.0)
        between = carry + jnp.dot(lf.astype(BF16), later, preferred_element_type=F32)
        w = jnp.exp((z - sp) + between)
        if diag:
            w = jnp.where(past, w, 0.0)
        pv = jnp.dot(w.astype(BF16), vs, preferred_element_type=F32)
        return pv, carry + jnp.sum(lf, axis=1, keepdims=True)

    outs = []
    for h in range(2):
        hm = (lane >= 64 * h) & (lane < 64 * (h + 1))
        qm = jnp.where(hm, q, jnp.zeros_like(q))
        pv0, carry0 = tile(qm, i, jnp.zeros((tq, 1), F32), True)

        def cond(st):
            j, m, _, _ = st
            return (j >= 0) & (m > -SB_EXP_UNDERFLOW)

        def body(st, qm=qm):
            j, _, carry, acc = st
            pv, carry = tile(qm, j, carry, False)
            return j - 1, jnp.max(carry), carry, acc + pv

        _, _, _, acc = lax.while_loop(cond, body, (i - 1, jnp.max(carry0), carry0, pv0))
        outs.append(acc)
    o_ref[0] = jnp.where(lane < 64, outs[0], outs[1]).astype(o_ref.dtype)


def _mla_attn_kernel(q_ref, k_ref, v_ref, o_ref, *, tq):
    i = pl.program_id(2)
    q = q_ref[0]
    lane = lax.broadcasted_iota(jnp.int32, (1, LANES), 1)
    lo = lane < 64
    l2 = lax.broadcasted_iota(jnp.int32, (1, PAIR_W), 1)
    lr = l2 & 63
    hm0 = (l2 < 64) | ((l2 >= LANES) & (lr < 16))
    hm1 = ((l2 >= 64) & (l2 < LANES)) | ((l2 >= LANES) & (lr >= 16) & (lr < 32))
    zq = jnp.zeros_like(q)
    qcat = jnp.concatenate([jnp.where(hm0, q, zq), jnp.where(hm1, q, zq)], axis=0)
    row = lax.broadcasted_iota(jnp.int32, (2 * tq, tq), 0) & (tq - 1)
    col = lax.broadcasted_iota(jnp.int32, (2 * tq, tq), 1)
    visible = (col // CHUNK) <= (row // CHUNK)

    def step(j, st, diag):
        m, l, acc = st
        start = pl.multiple_of(j * tq, tq)
        ks = k_ref[0, pl.ds(start, tq), :]
        vs = v_ref[0, pl.ds(start, tq), :]
        s = lax.dot_general(qcat, ks, _TRANS_B, preferred_element_type=F32)
        if diag:
            s = jnp.where(visible, s, -jnp.inf)
        m_new = jnp.maximum(m, jnp.max(s, axis=1, keepdims=True))
        alpha = jnp.exp(m - m_new)
        p = jnp.exp(s - m_new)
        l_new = alpha * l + jnp.sum(p, axis=1, keepdims=True)
        pb = p.astype(BF16)
        pcat = jnp.concatenate([pb[:tq], pb[tq:]], axis=1)
        zv = jnp.zeros_like(vs)
        vcat = jnp.concatenate([jnp.where(lo, vs, zv), jnp.where(lo, zv, vs)], axis=0)
        acc_new = (jnp.where(lo, alpha[:tq], alpha[tq:]) * acc
                   + jnp.dot(pcat, vcat, preferred_element_type=F32))
        return m_new, l_new, acc_new

    st = step(i, (jnp.full((2 * tq, 1), -jnp.inf, F32), jnp.zeros((2 * tq, 1), F32),
                  jnp.zeros((tq, LANES), F32)), True)
    _, l, acc = lax.fori_loop(0, i, lambda j, st: step(j, st, False), st)
    o_ref[0] = (acc / jnp.where(lo, l[:tq], l[tq:])).astype(o_ref.dtype)


def _post_attn_kernel(x_ref, sb_ref, mla_ref, gsb_ref, gmla_ref, wo_ref, gmem_ref, wmq_ref, gqh_ref,
                      mk_ref, mv_ref, wmo_ref, gmoe_ref, wr_hi_ref, wr_lo_ref, br_ref,
                      h2g_ref, dest_ref, cnt_ref, carry_ref, *, n_tokens):
    d = x_ref.shape[-1]
    hd = d // MEM_HEADS
    sbn = _rms(sb_ref[...].astype(F32), gsb_ref[...]).astype(BF16)
    mlan = _rms(mla_ref[...].astype(F32), gmla_ref[...]).astype(BF16)
    mixed = jnp.concatenate([sbn, mlan], axis=1)
    h1 = x_ref[...] + jnp.dot(mixed, wo_ref[...], preferred_element_type=F32)

    hq = _rms(h1, gmem_ref[...]).astype(BF16)
    mq = jnp.dot(hq, wmq_ref[...], preferred_element_type=F32)
    mos = []
    for h in range(MEM_HEADS):
        qh = (_rms(mq[:, h * hd:(h + 1) * hd], gqh_ref[...]) * (hd ** -0.5)).astype(BF16)
        sc = lax.dot_general(qh, mk_ref[0, :, h * hd:(h + 1) * hd], _TRANS_B, preferred_element_type=F32)
        sc = sc - jnp.max(sc, axis=-1, keepdims=True)
        e = jnp.exp(sc)
        p = (e / jnp.sum(e, axis=-1, keepdims=True)).astype(BF16)
        mos.append(jnp.dot(p, mv_ref[0, :, h * hd:(h + 1) * hd], preferred_element_type=F32).astype(BF16))
    h2 = h1 + jnp.dot(jnp.concatenate(mos, axis=1), wmo_ref[...], preferred_element_type=F32)
    h2g_ref[:, 0:d] = h2

    t = _rms(h2, gmoe_ref[...])
    t_hi = t.astype(BF16)
    t_lo = (t - t_hi.astype(F32)).astype(BF16)
    logits = (jnp.dot(t_hi, wr_hi_ref[...], preferred_element_type=F32)
              + jnp.dot(t_hi, wr_lo_ref[...], preferred_element_type=F32)
              + jnp.dot(t_lo, wr_hi_ref[...], preferred_element_type=F32)) + br_ref[...]

    lane = lax.broadcasted_iota(jnp.int32, (1, LANES), 1).astype(F32)
    big = float(LANES)
    neg = -jnp.inf
    lg = jnp.where(lane < N_GROUPS, logits, neg)
    gmax = jnp.max(lg, axis=-1, keepdims=True)
    g_idx = jnp.min(jnp.where(lg == gmax, lane, big), axis=-1, keepdims=True)
    g_w = 1.0 / jnp.sum(jnp.exp(lg - gmax), axis=-1, keepdims=True)
    e_lo = N_GROUPS + g_idx * EXPERTS_PER_GROUP
    in_grp = (lane >= e_lo) & (lane < e_lo + EXPERTS_PER_GROUP)
    le = jnp.where(in_grp, logits, neg)
    m1 = jnp.max(le, axis=-1, keepdims=True)
    i1 = jnp.min(jnp.where(le == m1, lane, big), axis=-1, keepdims=True)
    le2 = jnp.where(lane == i1, neg, le)
    m2 = jnp.max(le2, axis=-1, keepdims=True)
    i2 = jnp.min(jnp.where(le2 == m2, lane, big), axis=-1, keepdims=True)
    e2 = jnp.exp(m2 - m1)
    w1 = g_w / (1.0 + e2)
    w2 = g_w * e2 / (1.0 + e2)
    h2g_ref[:, d:] = jnp.where(lane == i1, w1, 0.0) + jnp.where(lane == i2, w2, 0.0)

    @pl.when(pl.program_id(0) == 0)
    def _():
        carry_ref[...] = jnp.zeros_like(carry_ref)

    tp = x_ref.shape[0]
    oh = jnp.where(lane == g_idx, 1.0, 0.0)
    earlier = (lax.broadcasted_iota(jnp.int32, (tp, tp), 1)
               < lax.broadcasted_iota(jnp.int32, (tp, tp), 0)).astype(BF16)
    prefix = jnp.dot(earlier, oh.astype(BF16), preferred_element_type=F32)
    carry = carry_ref[...]
    rank = jnp.sum(jnp.where(lane == g_idx, prefix + carry, 0.0), axis=-1, keepdims=True)
    dest_ref[...] = (g_idx * float(n_tokens) + rank).astype(jnp.int32)
    carry_ref[...] = carry + jnp.sum(oh, axis=0, keepdims=True)
    cnt_ref[0] = carry_ref[...]


def _scatter_rows_kernel(dest_ref, cnt_ref, x_ref, o_hbm, zrow_ref, sem, zsem, *, n_tokens, row_tile):
    i = pl.program_id(0)
    tg = x_ref.shape[0]
    base = i * tg

    def issue(r, c):
        pltpu.make_async_copy(x_ref.at[pl.ds(r, 1)], o_hbm.at[pl.ds(dest_ref[base + r], 1)], sem).start()
        return c

    lax.fori_loop(0, tg, issue, 0, unroll=8)
    pltpu.make_async_copy(x_ref, o_hbm.at[pl.ds(0, tg)], sem).wait()

    @pl.when(i == pl.num_programs(0) - 1)
    def _():
        zrow_ref[...] = jnp.zeros_like(zrow_ref)
        for g in range(N_GROUPS):
            c = cnt_ref[g]
            pad = (row_tile - (c & (row_tile - 1))) & (row_tile - 1)
            first = g * n_tokens + c
            zcopy = lambda r: pltpu.make_async_copy(zrow_ref, o_hbm.at[pl.ds(first + r, 1)], zsem)

            def zissue(r, k):
                zcopy(r).start()
                return k

            def zwait(r, k):
                zcopy(r).wait()
                return k

            lax.fori_loop(0, pad, zissue, 0)
            lax.fori_loop(0, pad, zwait, 0)


def _moe_group_kernel(blk_ref, grp_ref, nvalid_ref, x_ref, gmoe_ref, wg_ref, wu_ref, wd_ref, o_ref):
    i = pl.program_id(0)
    d = o_ref.shape[-1]

    @pl.when(i < nvalid_ref[0])
    def _():
        x = x_ref[...]
        h2 = x[:, 0:d]
        gates = x[:, d:]
        t = _rms(h2, gmoe_ref[...]).astype(BF16)
        lane = lax.broadcasted_iota(jnp.int32, (1, LANES), 1)
        first = N_GROUPS + grp_ref[i] * EXPERTS_PER_GROUP
        hs = []
        for k in range(EXPERTS_PER_GROUP):
            gk = jnp.sum(jnp.where(lane == first + k, gates, 0.0), axis=-1, keepdims=True)
            a = jnp.dot(t, wg_ref[0, k], preferred_element_type=F32)
            u = jnp.dot(t, wu_ref[0, k], preferred_element_type=F32)
            hs.append((a * jax.nn.sigmoid(a) * u * gk).astype(BF16))
        o_ref[...] = h2 + jnp.dot(jnp.concatenate(hs, axis=1), wd_ref[0], preferred_element_type=F32)


def _gather_rows_kernel(dest_ref, y_hbm, o_ref, sem):
    tg = o_ref.shape[0]
    base = pl.program_id(0) * tg

    def issue(r, c):
        pltpu.make_async_copy(y_hbm.at[pl.ds(dest_ref[base + r], 1)], o_ref.at[pl.ds(r, 1)], sem).start()
        return c

    lax.fori_loop(0, tg, issue, 0, unroll=8)
    pltpu.make_async_copy(y_hbm.at[pl.ds(0, tg)], o_ref, sem).wait()


def _rope_block(x1, x2):
    z = jnp.zeros(x1.shape[:-1] + (32,), x1.dtype)
    return jnp.concatenate([x1, x1, z, x2, x2, z], axis=-1)


def _pair_rope_block(a, b):
    z = jnp.zeros(a.shape[:-1] + (32,), a.dtype)
    return jnp.concatenate([a[..., :16], b[..., :16], z, a[..., 16:], b[..., 16:], z], axis=-1)


def _row(v):
    return v.reshape(1, -1).astype(F32)


def kernel(x, mem, positions, norm_mix_g, w_in, mla_q_norm_g, w_uq, mla_kv_norm_g, w_ukv, mla_qn_g, mla_qr_g, mla_kn_g, mla_kr_g, sb_out_g, mla_out_g, w_o, norm_mem_g, mem_src_g, w_mq, w_mkv, mem_qhead_g, mem_khead_g, w_mo, norm_moe_g, w_group, b_group, w_router, b_router, w_gate, w_up, w_down):
    B, S, D = x.shape
    T = B * S
    M = mem.shape[1]
    FF = w_gate.shape[-1]
    half = MLA_ROPE // 2
    qk = MLA_NOPE + MLA_ROPE

    o = 3 * SB_WIDTH + MLA_Q_RANK + MLA_KV_RANK
    w1 = jnp.concatenate([w_in[:, :SB_WIDTH] * (SB_HEAD_DIM ** -0.5), w_in[:, SB_WIDTH:o],
                          _rope_block(w_in[:, o:o + half], w_in[:, o + half:])], axis=1).astype(BF16)
    uq = w_uq.reshape(MLA_Q_RANK, MLA_HEADS, qk)
    ukv = w_ukv.reshape(MLA_KV_RANK, MLA_HEADS, MLA_NOPE + MLA_V)
    zk = jnp.zeros((MLA_KV_RANK, LANES), F32)
    uq_cols, uk_cols = [], []
    for p in range(N_PAIRS):
        a, b = 2 * p, 2 * p + 1
        uq_cols += [uq[:, a, :MLA_NOPE], uq[:, b, :MLA_NOPE], _pair_rope_block(uq[:, a, MLA_NOPE:], uq[:, b, MLA_NOPE:])]
        uk_cols += [ukv[:, a, :MLA_NOPE], ukv[:, b, :MLA_NOPE], zk]
    wuq = jnp.concatenate(uq_cols, axis=1).astype(BF16)
    wukv = jnp.concatenate(uk_cols + [ukv[:, h, MLA_NOPE:] for h in range(MLA_HEADS)], axis=1).astype(BF16)
    gq_pair = jnp.concatenate([mla_qn_g, mla_qn_g, _pair_rope_block(mla_qr_g, mla_qr_g)])
    gk_pair = jnp.concatenate([mla_kn_g, mla_kn_g, jnp.zeros((LANES,), F32)])
    gqcat = _row(jnp.tile(gq_pair, N_PAIRS))
    gkcat = _row(jnp.tile(gk_pair, N_PAIRS))
    gkr = _row(_rope_block(mla_kr_g[:half], mla_kr_g[half:]))
    inv_freq = ROPE_THETA ** (-(jnp.arange(half, dtype=F32) * 2.0 / MLA_ROPE))
    invf = _row(_rope_block(inv_freq, inv_freq))
    sgn = _row(jnp.concatenate([-jnp.ones((64,), F32), jnp.ones((64,), F32)]))
    wr = jnp.concatenate([w_group, w_router, jnp.zeros((D, LANES - N_GROUPS - N_EXPERTS), F32)], axis=1)
    wr_hi = wr.astype(BF16)
    wr_lo = (wr - wr_hi.astype(F32)).astype(BF16)
    br = _row(jnp.concatenate([b_group, b_router, jnp.zeros((LANES - N_GROUPS - N_EXPERTS,), F32)]))

    full = lambda shape: pl.BlockSpec(shape, lambda *_: (0,) * len(shape))

    mk, mv = pl.pallas_call(
        _mem_kv_kernel, grid=(B,), name="mem_kv",
        in_specs=[pl.BlockSpec((1, M, D), lambda b: (b, 0, 0)), full((1, D)), full((D, 2 * D)),
                  full((1, D // MEM_HEADS))],
        out_specs=[pl.BlockSpec((1, M, D), lambda b: (b, 0, 0))] * 2,
        out_shape=[jax.ShapeDtypeStruct((B, M, D), BF16)] * 2,
    )(mem, _row(mem_src_g), w_mkv.astype(BF16), _row(mem_khead_g))

    tm = 256
    rows = lambda w: pl.BlockSpec((tm, w), lambda i: (i, 0))
    sbq, sbk, sbv, mq, mkk, mvv = pl.pallas_call(
        _in_proj_kernel, grid=(T // tm,), name="in_proj",
        in_specs=[rows(D), rows(1), full((1, D)), full(w1.shape), full((1, MLA_Q_RANK)), full(wuq.shape),
                  full((1, MLA_KV_RANK)), full(wukv.shape), full(gqcat.shape), full(gkcat.shape),
                  full((1, LANES)), full((1, LANES)), full((1, LANES))],
        out_specs=[rows(SB_WIDTH)] * 3 + [rows(N_PAIRS * PAIR_W)] * 2 + [rows(MLA_WIDTH)],
        out_shape=[jax.ShapeDtypeStruct((T, SB_WIDTH), BF16)] * 3
                  + [jax.ShapeDtypeStruct((T, N_PAIRS * PAIR_W), BF16)] * 2
                  + [jax.ShapeDtypeStruct((T, MLA_WIDTH), BF16)],
    )(x.reshape(T, D), positions.reshape(T, 1), _row(norm_mix_g), w1, _row(mla_q_norm_g), wuq,
      _row(mla_kv_norm_g), wukv, gqcat, gkcat, gkr, invf, sgn)

    qspec = lambda t, w: pl.BlockSpec((1, t, w), lambda b, p, i: (b, i, p))
    kvspec = lambda w: pl.BlockSpec((1, S, w), lambda b, p, i: (b, 0, p))
    tq = 256
    sb = pl.pallas_call(
        functools.partial(_sb_attn_kernel, tq=tq), grid=(B, SB_HEADS // 2, S // tq), name="sb_attn",
        in_specs=[qspec(tq, LANES), kvspec(LANES), kvspec(LANES)],
        out_specs=qspec(tq, LANES),
        out_shape=jax.ShapeDtypeStruct((B, S, SB_WIDTH), BF16),
    )(sbq.reshape(B, S, SB_WIDTH), sbk.reshape(B, S, SB_WIDTH), sbv.reshape(B, S, SB_WIDTH))
    tqm = 512
    mla = pl.pallas_call(
        functools.partial(_mla_attn_kernel, tq=tqm), grid=(B, N_PAIRS, S // tqm), name="mla_attn",
        in_specs=[qspec(tqm, PAIR_W), kvspec(PAIR_W), kvspec(LANES)],
        out_specs=qspec(tqm, LANES),
        out_shape=jax.ShapeDtypeStruct((B, S, MLA_WIDTH), BF16),
    )(mq.reshape(B, S, -1), mkk.reshape(B, S, -1), mvv.reshape(B, S, MLA_WIDTH))

    tp = 256
    prow = lambda w: pl.BlockSpec((tp, w), lambda i: (i, 0))
    memspec = pl.BlockSpec((1, M, D), lambda i: ((i * tp) // S, 0, 0))
    W = D + LANES
    h2g, dest, cnt = pl.pallas_call(
        functools.partial(_post_attn_kernel, n_tokens=T), grid=(T // tp,), name="post_attn",
        in_specs=[prow(D), prow(SB_WIDTH), prow(MLA_WIDTH), full((1, SB_WIDTH)), full((1, MLA_WIDTH)),
                  full((D, D)), full((1, D)), full((D, D)), full((1, D // MEM_HEADS)), memspec, memspec,
                  full((D, D)), full((1, D)), full((D, LANES)), full((D, LANES)), full((1, LANES))],
        out_specs=[prow(W), prow(1), pl.BlockSpec((1, 1, LANES), lambda i: (i, 0, 0))],
        out_shape=[jax.ShapeDtypeStruct((T, W), F32), jax.ShapeDtypeStruct((T, 1), jnp.int32),
                   jax.ShapeDtypeStruct((T // tp, 1, LANES), F32)],
        scratch_shapes=[pltpu.VMEM((1, LANES), F32)],
        compiler_params=pltpu.CompilerParams(dimension_semantics=("arbitrary",)),
    )(x.reshape(T, D), sb.reshape(T, SB_WIDTH), mla.reshape(T, MLA_WIDTH), _row(sb_out_g), _row(mla_out_g),
      w_o.astype(BF16), _row(norm_mem_g), w_mq.astype(BF16), _row(mem_qhead_g), mk, mv,
      w_mo.astype(BF16), _row(norm_moe_g), wr_hi, wr_lo, br)

    te = 256
    tg = 512
    dest = dest.reshape(T)
    counts = cnt[-1, 0, :N_GROUPS].astype(jnp.int32)
    tiles = (counts + te - 1) // te
    ends = jnp.cumsum(tiles)
    n_valid = ends[-1]
    n_steps = T // te + N_GROUPS
    step = jnp.minimum(jnp.arange(n_steps, dtype=jnp.int32), n_valid - 1)
    grp = jnp.sum(step[:, None] >= ends[None, :], axis=1).astype(jnp.int32)
    blk = grp * (T // te) + step - (ends - tiles)[grp]

    any_spec = pl.BlockSpec(memory_space=pl.ANY)
    h2s = pl.pallas_call(
        functools.partial(_scatter_rows_kernel, n_tokens=T, row_tile=te), name="moe_dispatch",
        grid_spec=pltpu.PrefetchScalarGridSpec(
            num_scalar_prefetch=2, grid=(T // tg,),
            in_specs=[pl.BlockSpec((tg, W), lambda i, *_: (i, 0))],
            out_specs=any_spec,
            scratch_shapes=[pltpu.VMEM((1, W), F32), pltpu.SemaphoreType.DMA, pltpu.SemaphoreType.DMA]),
        out_shape=jax.ShapeDtypeStruct((N_GROUPS * T, W), F32),
        compiler_params=pltpu.CompilerParams(dimension_semantics=("arbitrary",)),
    )(dest, counts, h2g)

    ys = pl.pallas_call(
        _moe_group_kernel, name="moe_experts",
        grid_spec=pltpu.PrefetchScalarGridSpec(
            num_scalar_prefetch=3, grid=(n_steps,),
            in_specs=[pl.BlockSpec((te, W), lambda i, blk, grp, nv: (blk[i], 0)),
                      pl.BlockSpec((1, D), lambda i, *_: (0, 0)),
                      pl.BlockSpec((1, EXPERTS_PER_GROUP, D, FF), lambda i, blk, grp, nv: (grp[i], 0, 0, 0)),
                      pl.BlockSpec((1, EXPERTS_PER_GROUP, D, FF), lambda i, blk, grp, nv: (grp[i], 0, 0, 0)),
                      pl.BlockSpec((1, EXPERTS_PER_GROUP * FF, D), lambda i, blk, grp, nv: (grp[i], 0, 0))],
            out_specs=pl.BlockSpec((te, D), lambda i, blk, grp, nv: (blk[i], 0))),
        out_shape=jax.ShapeDtypeStruct((N_GROUPS * T, D), F32),
        compiler_params=pltpu.CompilerParams(dimension_semantics=("arbitrary",)),
    )(blk, grp, n_valid.reshape(1), h2s, _row(norm_moe_g),
      w_gate.astype(BF16).reshape(N_GROUPS, EXPERTS_PER_GROUP, D, FF),
      w_up.astype(BF16).reshape(N_GROUPS, EXPERTS_PER_GROUP, D, FF),
      w_down.astype(BF16).reshape(N_GROUPS, EXPERTS_PER_GROUP * FF, D))

    out = pl.pallas_call(
        _gather_rows_kernel, name="moe_combine",
        grid_spec=pltpu.PrefetchScalarGridSpec(
            num_scalar_prefetch=1, grid=(T // tg,),
            in_specs=[any_spec],
            out_specs=pl.BlockSpec((tg, D), lambda i, *_: (i, 0)),
            scratch_shapes=[pltpu.SemaphoreType.DMA]),
        out_shape=jax.ShapeDtypeStruct((T, D), F32),
        compiler_params=pltpu.CompilerParams(dimension_semantics=("arbitrary",)),
    )(dest, ys)
    return out.reshape(B, S, D)
```

```python
import functools

import numpy as np
import jax
import jax.numpy as jnp
from jax import lax
from jax.experimental import pallas as pl
from jax.experimental.pallas import tpu as pltpu

F32 = jnp.float32
BF16 = jnp.bfloat16
EPS = 1e-6
LANES = 128
SUBLANES = 8

CHUNK = 64
SB_HEADS = 8
SB_HEAD_DIM = 64
SB_WIDTH = SB_HEADS * SB_HEAD_DIM
MLA_HEADS = 8
MLA_NOPE = 64
MLA_ROPE = 32
MLA_V = 64
MLA_WIDTH = MLA_HEADS * MLA_V
MLA_Q_RANK = 384
MLA_KV_RANK = 256
ROPE_THETA = 10000.0
MEM_HEADS = 4
N_GROUPS = 4
EXPERTS_PER_GROUP = 4
N_EXPERTS = N_GROUPS * EXPERTS_PER_GROUP
PAIRS_PER_GROUP = EXPERTS_PER_GROUP * (EXPERTS_PER_GROUP - 1) // 2
N_PAIRS = MLA_HEADS // 2
PAIR_W = 2 * LANES
SB_EXP_UNDERFLOW = 104.0
SB_DROPPED = -1e30
LOG2E = 1.4426950408889634
MLA_SAFE_LOGIT = 60.0

_TRANS_B = (((1,), (1,)), ((), ()))


def _rms(x, g):
    return x * lax.rsqrt(jnp.mean(x * x, axis=-1, keepdims=True) + EPS) * g


def _seg_rms(x, masks, seg_len, g):
    x2 = x * x
    r = jnp.zeros_like(x)
    for m in masks:
        s = jnp.sum(jnp.where(m, x2, 0.0), axis=-1, keepdims=True)
        r = jnp.where(m, lax.rsqrt(s * (1.0 / seg_len) + EPS), r)
    return x * r * g


def _mem_kv_kernel(mem_ref, gsrc_ref, w_ref, gk_ref, mk_ref, mv_ref):
    d = mem_ref.shape[-1]
    hd = d // MEM_HEADS
    mn = _rms(mem_ref[0], gsrc_ref[...]).astype(BF16)
    kv = jnp.dot(mn, w_ref[...], preferred_element_type=F32)
    for h in range(MEM_HEADS):
        kh = kv[:, h * hd:(h + 1) * hd]
        mk_ref[0, :, h * hd:(h + 1) * hd] = _rms(kh, gk_ref[...]).astype(BF16)
    mv_ref[0] = kv[:, d:].astype(BF16)


def _in_proj_kernel(x_ref, pos_ref, gmix_ref, w1_ref, gq_ref, wuq_ref, gkv_ref, wukv_ref,
                    gqcat_ref, gkcat_ref, gkr_ref, invf_ref, sgn_ref,
                    sbq_ref, sbk_ref, sbv_ref, mq_ref, mk_ref, mv_ref):
    hn = _rms(x_ref[...], gmix_ref[...]).astype(BF16)
    proj = jnp.dot(hn, w1_ref[...], preferred_element_type=F32)
    sbq_ref[...] = proj[:, 0:SB_WIDTH].astype(BF16)
    sbk_ref[...] = proj[:, SB_WIDTH:2 * SB_WIDTH].astype(BF16)
    sbv_ref[...] = proj[:, 2 * SB_WIDTH:3 * SB_WIDTH].astype(BF16)
    o = 3 * SB_WIDTH
    cq = proj[:, o:o + MLA_Q_RANK]
    ckv = proj[:, o + MLA_Q_RANK:o + MLA_Q_RANK + MLA_KV_RANK]
    kr = proj[:, o + MLA_Q_RANK + MLA_KV_RANK:]

    q = jnp.dot(_rms(cq, gq_ref[...]).astype(BF16), wuq_ref[...], preferred_element_type=F32)
    kv = jnp.dot(_rms(ckv, gkv_ref[...]).astype(BF16), wukv_ref[...], preferred_element_type=F32)

    lane = lax.broadcasted_iota(jnp.int32, (1, LANES), 1)
    lr = lane & 63
    lo, hi = lane < 64, lane >= 64
    r0, r1 = lr < 16, (lr >= 16) & (lr < 32)

    ang = pos_ref[...].astype(F32) * invf_ref[...]
    cos_t = jnp.cos(ang)
    sin_t = jnp.sin(ang) * sgn_ref[...]

    def rope(b):
        return b * cos_t + pltpu.roll(b, 64, 1) * sin_t

    q_scale = (MLA_NOPE + MLA_ROPE) ** -0.5 * LOG2E
    krn = rope(_seg_rms(kr, (r0, r1), MLA_ROPE, gkr_ref[...]))
    for p in range(N_PAIRS):
        b0 = p * PAIR_W
        qa = _seg_rms(q[:, b0:b0 + LANES], (lo, hi), MLA_NOPE, gqcat_ref[:, b0:b0 + LANES])
        qb = rope(_seg_rms(q[:, b0 + LANES:b0 + PAIR_W], (r0, r1), MLA_ROPE,
                           gqcat_ref[:, b0 + LANES:b0 + PAIR_W]))
        mq_ref[:, b0:b0 + LANES] = (qa * q_scale).astype(BF16)
        mq_ref[:, b0 + LANES:b0 + PAIR_W] = (qb * q_scale).astype(BF16)
        ka = _seg_rms(kv[:, b0:b0 + LANES], (lo, hi), MLA_NOPE, gkcat_ref[:, b0:b0 + LANES])
        mk_ref[:, b0:b0 + LANES] = ka.astype(BF16)
        mk_ref[:, b0 + LANES:b0 + PAIR_W] = krn.astype(BF16)
    mv_ref[...] = kv[:, N_PAIRS * PAIR_W:].astype(BF16)


def _sb_attn_kernel(q_ref, k_ref, v_ref, o_ref, *, tq):
    i = pl.program_id(2)
    q = q_ref[0]
    lane = lax.broadcasted_iota(jnp.int32, (1, LANES), 1)
    lo = lane < 64
    zq = jnp.zeros_like(q)
    qcat = jnp.concatenate([jnp.where(lo, q, zq), jnp.where(lo, zq, q)], axis=0)
    row = lax.broadcasted_iota(jnp.int32, (tq, tq), 0)
    col = lax.broadcasted_iota(jnp.int32, (tq, tq), 1)
    later = (row > col).astype(BF16)
    past = jnp.concatenate([col < row, col < row], axis=0)

    def scores(j, keep=None):
        start = pl.multiple_of(j * tq, tq)
        z = lax.dot_general(qcat, k_ref[0, pl.ds(start, tq), :], _TRANS_B, preferred_element_type=F32)
        if keep is not None:
            z = jnp.where(keep, z, SB_DROPPED)
        return z, jnp.maximum(z, 0.0) + jnp.log(1.0 + jnp.exp(-jnp.abs(z)))

    def weights(z, fail, after):
        between = after + jnp.dot(fail.astype(BF16), later, preferred_element_type=F32)
        return jnp.exp((z - fail) - between).astype(BF16)

    def pv(ws, js):
        zv = jnp.zeros((tq, LANES), BF16)
        vs = [v_ref[0, pl.ds(pl.multiple_of(j * tq, tq), tq), :] for j in js]
        wcat = jnp.concatenate([w[:tq] for w in ws] + [w[tq:] for w in ws], axis=1)
        vcat = jnp.concatenate([jnp.where(lo, v, zv) for v in vs] + [jnp.where(lo, zv, v) for v in vs], axis=0)
        return jnp.dot(wcat, vcat, preferred_element_type=F32)

    prev = jnp.maximum(i - 1, 0)
    zb, fb = scores(i, past)
    tot_b = jnp.sum(fb, axis=1, keepdims=True)
    wb = weights(zb, fb, 0.0)
    za, fa = scores(prev, i > 0)
    wa = weights(za, fa, tot_b)
    carry0 = tot_b + jnp.sum(fa, axis=1, keepdims=True)
    acc0 = pv([wa, wb], [prev, i])

    def cond(st):
        j, m, _, _ = st
        return (j >= 0) & (m < SB_EXP_UNDERFLOW)

    def body(st):
        j, _, carry, acc = st
        z, fail = scores(j)
        w = weights(z, fail, carry)
        carry = carry + jnp.sum(fail, axis=1, keepdims=True)
        return j - 1, jnp.min(carry), carry, acc + pv([w], [j])

    _, _, _, acc = lax.while_loop(cond, body, (i - 2, jnp.min(carry0), carry0, acc0))
    o_ref[0] = acc.astype(o_ref.dtype)


def _mla_attn_kernel(bounded_ref, q_ref, k_ref, v_ref, o_ref, l_ref, acc_ref, *, tq):
    i = pl.program_id(2)
    q = q_ref[0]
    lane = lax.broadcasted_iota(jnp.int32, (1, LANES), 1)
    lo = lane < 64
    l2 = lax.broadcasted_iota(jnp.int32, (1, PAIR_W), 1)
    lr = l2 & 63
    hm0 = (l2 < 64) | ((l2 >= LANES) & (lr < 16))
    hm1 = ((l2 >= 64) & (l2 < LANES)) | ((l2 >= LANES) & (lr >= 16) & (lr < 32))
    zq = jnp.zeros_like(q)
    qcat = jnp.concatenate([jnp.where(hm0, q, zq), jnp.where(hm1, q, zq)], axis=0)
    row = lax.broadcasted_iota(jnp.int32, (2 * tq, tq), 0) & (tq - 1)
    col = lax.broadcasted_iota(jnp.int32, (2 * tq, tq), 1)
    visible = (col // CHUNK) <= (row // CHUNK)

    def logits(j, n, diag):
        start = pl.multiple_of(j * tq, tq)
        s = lax.dot_general(qcat, k_ref[0, pl.ds(start, n * tq), :], _TRANS_B, preferred_element_type=F32)
        if diag:
            vis = visible if n == 1 else jnp.concatenate([jnp.ones_like(visible)] * (n - 1) + [visible], axis=1)
            s = jnp.where(vis, s, -jnp.inf)
        return s

    def pv(j, n, p):
        vs = v_ref[0, pl.ds(pl.multiple_of(j * tq, tq), n * tq), :]
        pb = p.astype(BF16)
        zv = jnp.zeros_like(vs)
        return jnp.dot(jnp.concatenate([pb[:tq], pb[tq:]], axis=1),
                       jnp.concatenate([jnp.where(lo, vs, zv), jnp.where(lo, zv, vs)], axis=0),
                       preferred_element_type=F32)

    def finish(l, acc):
        o_ref[0] = (acc / jnp.where(lo, l[:tq], l[tq:])).astype(o_ref.dtype)

    @pl.when(bounded_ref[0] != 0)
    def _():
        def step(j, n, diag):
            p = jnp.exp2(logits(j, n, diag))
            l_ref[...] += jnp.sum(p, axis=1, keepdims=True)
            acc_ref[...] += pv(j, n, p)

        l_ref[...] = jnp.zeros_like(l_ref)
        acc_ref[...] = jnp.zeros_like(acc_ref)

        @pl.loop(0, i // 2)
        def _(t):
            step(2 * t, 2, False)

        @pl.when(i % 2 == 1)
        def _():
            step(i - 1, 2, True)

        @pl.when(i % 2 == 0)
        def _():
            step(i, 1, True)

        finish(l_ref[...], acc_ref[...])

    @pl.when(bounded_ref[0] == 0)
    def _():
        def step(j, st, diag):
            m, l, acc = st
            s = logits(j, 1, diag)
            m_new = jnp.maximum(m, jnp.max(s, axis=1, keepdims=True))
            alpha = jnp.exp2(m - m_new)
            p = jnp.exp2(s - m_new)
            l_new = alpha * l + jnp.sum(p, axis=1, keepdims=True)
            return m_new, l_new, jnp.where(lo, alpha[:tq], alpha[tq:]) * acc + pv(j, 1, p)

        st = step(i, (jnp.full((2 * tq, 1), -jnp.inf, F32), jnp.zeros((2 * tq, 1), F32),
                      jnp.zeros((tq, LANES), F32)), True)
        _, l, acc = lax.fori_loop(0, i, lambda j, st: step(j, st, False), st)
        finish(l, acc)


def _post_attn_kernel(x_ref, sb_ref, mla_ref, gsb_ref, gmla_ref, wo_ref, gmem_ref, wmq_ref, gqh_ref,
                      mk_ref, mv_ref, wmo_ref, gmoe_ref, wr_hi_ref, wr_lo_ref, br_ref,
                      h2g_ref, dest_ref, cnt_ref, carry_ref, *, n_tokens):
    d = x_ref.shape[-1]
    hd = d // MEM_HEADS
    sbn = _rms(sb_ref[...].astype(F32), gsb_ref[...]).astype(BF16)
    mlan = _rms(mla_ref[...].astype(F32), gmla_ref[...]).astype(BF16)
    mixed = jnp.concatenate([sbn, mlan], axis=1)
    h1 = x_ref[...] + jnp.dot(mixed, wo_ref[...], preferred_element_type=F32)

    hq = _rms(h1, gmem_ref[...]).astype(BF16)
    mq = jnp.dot(hq, wmq_ref[...], preferred_element_type=F32)
    mos = []
    for h in range(MEM_HEADS):
        qh = (_rms(mq[:, h * hd:(h + 1) * hd], gqh_ref[...]) * (hd ** -0.5)).astype(BF16)
        sc = lax.dot_general(qh, mk_ref[0, :, h * hd:(h + 1) * hd], _TRANS_B, preferred_element_type=F32)
        sc = sc - jnp.max(sc, axis=-1, keepdims=True)
        e = jnp.exp(sc)
        p = (e / jnp.sum(e, axis=-1, keepdims=True)).astype(BF16)
        mos.append(jnp.dot(p, mv_ref[0, :, h * hd:(h + 1) * hd], preferred_element_type=F32).astype(BF16))
    h2 = h1 + jnp.dot(jnp.concatenate(mos, axis=1), wmo_ref[...], preferred_element_type=F32)
    h2g_ref[:, 0:d] = h2

    t = _rms(h2, gmoe_ref[...])
    t_hi = t.astype(BF16)
    t_lo = (t - t_hi.astype(F32)).astype(BF16)
    logits = (jnp.dot(t_hi, wr_hi_ref[...], preferred_element_type=F32)
              + jnp.dot(t_hi, wr_lo_ref[...], preferred_element_type=F32)
              + jnp.dot(t_lo, wr_hi_ref[...], preferred_element_type=F32)) + br_ref[...]

    lane = lax.broadcasted_iota(jnp.int32, (1, LANES), 1).astype(F32)
    big = float(LANES)
    neg = -jnp.inf
    lg = jnp.where(lane < N_GROUPS, logits, neg)
    gmax = jnp.max(lg, axis=-1, keepdims=True)
    g_idx = jnp.min(jnp.where(lg == gmax, lane, big), axis=-1, keepdims=True)
    g_w = 1.0 / jnp.sum(jnp.exp(lg - gmax), axis=-1, keepdims=True)
    e_lo = N_GROUPS + g_idx * EXPERTS_PER_GROUP
    in_grp = (lane >= e_lo) & (lane < e_lo + EXPERTS_PER_GROUP)
    le = jnp.where(in_grp, logits, neg)
    m1 = jnp.max(le, axis=-1, keepdims=True)
    i1 = jnp.min(jnp.where(le == m1, lane, big), axis=-1, keepdims=True)
    le2 = jnp.where(lane == i1, neg, le)
    m2 = jnp.max(le2, axis=-1, keepdims=True)
    i2 = jnp.min(jnp.where(le2 == m2, lane, big), axis=-1, keepdims=True)
    e2 = jnp.exp(m2 - m1)
    w1 = g_w / (1.0 + e2)
    w2 = g_w * e2 / (1.0 + e2)
    h2g_ref[:, d:] = jnp.where(lane == i1, w1, 0.0) + jnp.where(lane == i2, w2, 0.0)

    @pl.when(pl.program_id(0) == 0)
    def _():
        carry_ref[...] = jnp.zeros_like(carry_ref)

    ka = jnp.minimum(i1, i2) - e_lo
    kb = jnp.maximum(i1, i2) - e_lo
    pair = ka * (5.0 - ka) * 0.5 + kb - 1.0
    bucket = g_idx * float(PAIRS_PER_GROUP) + pair
    tp = x_ref.shape[0]
    oh = jnp.where(lane == bucket, 1.0, 0.0)
    earlier = (lax.broadcasted_iota(jnp.int32, (tp, tp), 1)
               < lax.broadcasted_iota(jnp.int32, (tp, tp), 0)).astype(BF16)
    prefix = jnp.dot(earlier, oh.astype(BF16), preferred_element_type=F32)
    carry = carry_ref[...]
    rank = jnp.sum(jnp.where(lane == bucket, prefix + carry, 0.0), axis=-1, keepdims=True)
    dest_ref[...] = (bucket * float(n_tokens) + rank).astype(jnp.int32)
    carry_ref[...] = carry + jnp.sum(oh, axis=0, keepdims=True)
    cnt_ref[0] = carry_ref[...]


def _scatter_rows_kernel(dest_ref, fill_start_ref, fill_n_ref, x_ref, o_hbm, zrow_ref, sem, zsem):
    i = pl.program_id(0)
    tg = x_ref.shape[0]
    base = i * tg

    @pl.loop(0, tg // SUBLANES)
    def _(r8):
        r0 = pl.multiple_of(r8 * SUBLANES, SUBLANES)
        for k in range(SUBLANES):
            pltpu.make_async_copy(x_ref.at[pl.ds(r0 + k, 1)],
                                  o_hbm.at[pl.ds(dest_ref[base + r0 + k], 1)], sem).start()

    pltpu.make_async_copy(x_ref, o_hbm.at[pl.ds(0, tg)], sem).wait()

    @pl.when(i == pl.num_programs(0) - 1)
    def _():
        zrow_ref[...] = jnp.zeros_like(zrow_ref)

        @pl.loop(0, fill_n_ref.shape[0])
        def _(b):
            first = fill_start_ref[b]
            zcopy = lambda r: pltpu.make_async_copy(zrow_ref, o_hbm.at[pl.ds(first + r, 1)], zsem)

            @pl.loop(0, fill_n_ref[b])
            def _(r):
                zcopy(r).start()

            @pl.loop(0, fill_n_ref[b])
            def _(r):
                zcopy(r).wait()


def _moe_pair_kernel(blk_ref, ea_ref, eb_ref, nvalid_ref, x_ref, gmoe_ref,
                     wga_ref, wua_ref, wda_ref, wgb_ref, wub_ref, wdb_ref, o_ref):
    i = pl.program_id(0)
    d = o_ref.shape[-1]

    @pl.when(i < nvalid_ref[0])
    def _():
        x = x_ref[...]
        h2 = x[:, 0:d]
        gates = x[:, d:]
        t = _rms(h2, gmoe_ref[...]).astype(BF16)
        lane = lax.broadcasted_iota(jnp.int32, (1, LANES), 1)

        def hidden(e, wg_ref, wu_ref):
            ge = jnp.sum(jnp.where(lane == N_GROUPS + e, gates, 0.0), axis=-1, keepdims=True)
            a = jnp.dot(t, wg_ref[0], preferred_element_type=F32)
            u = jnp.dot(t, wu_ref[0], preferred_element_type=F32)
            return (a * jax.nn.sigmoid(a) * u * ge).astype(BF16)

        o_ref[...] = (h2 + jnp.dot(hidden(ea_ref[i], wga_ref, wua_ref), wda_ref[0], preferred_element_type=F32)
                      + jnp.dot(hidden(eb_ref[i], wgb_ref, wub_ref), wdb_ref[0], preferred_element_type=F32))


def _gather_rows_kernel(dest_ref, y_hbm, o_ref, sem):
    tg = o_ref.shape[0]
    base = pl.program_id(0) * tg

    @pl.loop(0, tg // SUBLANES)
    def _(r8):
        r0 = pl.multiple_of(r8 * SUBLANES, SUBLANES)
        for k in range(SUBLANES):
            pltpu.make_async_copy(y_hbm.at[pl.ds(dest_ref[base + r0 + k], 1)],
                                  o_ref.at[pl.ds(r0 + k, 1)], sem).start()

    pltpu.make_async_copy(y_hbm.at[pl.ds(0, tg)], o_ref, sem).wait()


def _rope_block(x1, x2):
    z = jnp.zeros(x1.shape[:-1] + (32,), x1.dtype)
    return jnp.concatenate([x1, x1, z, x2, x2, z], axis=-1)


def _pair_rope_block(a, b):
    z = jnp.zeros(a.shape[:-1] + (32,), a.dtype)
    return jnp.concatenate([a[..., :16], b[..., :16], z, a[..., 16:], b[..., 16:], z], axis=-1)


def _row(v):
    return v.reshape(1, -1).astype(F32)


def kernel(x, mem, positions, norm_mix_g, w_in, mla_q_norm_g, w_uq, mla_kv_norm_g, w_ukv, mla_qn_g, mla_qr_g, mla_kn_g, mla_kr_g, sb_out_g, mla_out_g, w_o, norm_mem_g, mem_src_g, w_mq, w_mkv, mem_qhead_g, mem_khead_g, w_mo, norm_moe_g, w_group, b_group, w_router, b_router, w_gate, w_up, w_down):
    B, S, D = x.shape
    T = B * S
    M = mem.shape[1]
    FF = w_gate.shape[-1]
    half = MLA_ROPE // 2
    qk = MLA_NOPE + MLA_ROPE

    o = 3 * SB_WIDTH + MLA_Q_RANK + MLA_KV_RANK
    w1 = jnp.concatenate([w_in[:, :SB_WIDTH] * (SB_HEAD_DIM ** -0.5), w_in[:, SB_WIDTH:o],
                          _rope_block(w_in[:, o:o + half], w_in[:, o + half:])], axis=1).astype(BF16)
    uq = w_uq.reshape(MLA_Q_RANK, MLA_HEADS, qk)
    ukv = w_ukv.reshape(MLA_KV_RANK, MLA_HEADS, MLA_NOPE + MLA_V)
    zk = jnp.zeros((MLA_KV_RANK, LANES), F32)
    uq_cols, uk_cols = [], []
    for p in range(N_PAIRS):
        a, b = 2 * p, 2 * p + 1
        uq_cols += [uq[:, a, :MLA_NOPE], uq[:, b, :MLA_NOPE], _pair_rope_block(uq[:, a, MLA_NOPE:], uq[:, b, MLA_NOPE:])]
        uk_cols += [ukv[:, a, :MLA_NOPE], ukv[:, b, :MLA_NOPE], zk]
    wuq = jnp.concatenate(uq_cols, axis=1).astype(BF16)
    wukv = jnp.concatenate(uk_cols + [ukv[:, h, MLA_NOPE:] for h in range(MLA_HEADS)], axis=1).astype(BF16)
    gq_pair = jnp.concatenate([mla_qn_g, mla_qn_g, _pair_rope_block(mla_qr_g, mla_qr_g)])
    gk_pair = jnp.concatenate([mla_kn_g, mla_kn_g, jnp.zeros((LANES,), F32)])
    gqcat = _row(jnp.tile(gq_pair, N_PAIRS))
    gkcat = _row(jnp.tile(gk_pair, N_PAIRS))
    gkr = _row(_rope_block(mla_kr_g[:half], mla_kr_g[half:]))
    inv_freq = ROPE_THETA ** (-(jnp.arange(half, dtype=F32) * 2.0 / MLA_ROPE))
    invf = _row(_rope_block(inv_freq, inv_freq))
    sgn = _row(jnp.concatenate([-jnp.ones((64,), F32), jnp.ones((64,), F32)]))
    wr = jnp.concatenate([w_group, w_router, jnp.zeros((D, LANES - N_GROUPS - N_EXPERTS), F32)], axis=1)
    wr_hi = wr.astype(BF16)
    wr_lo = (wr - wr_hi.astype(F32)).astype(BF16)
    br = _row(jnp.concatenate([b_group, b_router, jnp.zeros((LANES - N_GROUPS - N_EXPERTS,), F32)]))

    full = lambda shape: pl.BlockSpec(shape, lambda *_: (0,) * len(shape))

    mk, mv = pl.pallas_call(
        _mem_kv_kernel, grid=(B,), name="mem_kv",
        in_specs=[pl.BlockSpec((1, M, D), lambda b: (b, 0, 0)), full((1, D)), full((D, 2 * D)),
                  full((1, D // MEM_HEADS))],
        out_specs=[pl.BlockSpec((1, M, D), lambda b: (b, 0, 0))] * 2,
        out_shape=[jax.ShapeDtypeStruct((B, M, D), BF16)] * 2,
    )(mem, _row(mem_src_g), w_mkv.astype(BF16), _row(mem_khead_g))

    tm = 256
    rows = lambda w: pl.BlockSpec((tm, w), lambda i: (i, 0))
    sbq, sbk, sbv, mq, mkk, mvv = pl.pallas_call(
        _in_proj_kernel, grid=(T // tm,), name="in_proj",
        in_specs=[rows(D), rows(1), full((1, D)), full(w1.shape), full((1, MLA_Q_RANK)), full(wuq.shape),
                  full((1, MLA_KV_RANK)), full(wukv.shape), full(gqcat.shape), full(gkcat.shape),
                  full((1, LANES)), full((1, LANES)), full((1, LANES))],
        out_specs=[rows(SB_WIDTH)] * 3 + [rows(N_PAIRS * PAIR_W)] * 2 + [rows(MLA_WIDTH)],
        out_shape=[jax.ShapeDtypeStruct((T, SB_WIDTH), BF16)] * 3
                  + [jax.ShapeDtypeStruct((T, N_PAIRS * PAIR_W), BF16)] * 2
                  + [jax.ShapeDtypeStruct((T, MLA_WIDTH), BF16)],
    )(x.reshape(T, D), positions.reshape(T, 1), _row(norm_mix_g), w1, _row(mla_q_norm_g), wuq,
      _row(mla_kv_norm_g), wukv, gqcat, gkcat, gkr, invf, sgn)

    qspec = lambda t, w: pl.BlockSpec((1, t, w), lambda b, p, i: (b, i, p))
    kvspec = lambda w: pl.BlockSpec((1, S, w), lambda b, p, i: (b, 0, p))
    tq = 256
    sb = pl.pallas_call(
        functools.partial(_sb_attn_kernel, tq=tq), grid=(B, SB_HEADS // 2, S // tq), name="sb_attn",
        in_specs=[qspec(tq, LANES), kvspec(LANES), kvspec(LANES)],
        out_specs=qspec(tq, LANES),
        out_shape=jax.ShapeDtypeStruct((B, S, SB_WIDTH), BF16),
    )(sbq.reshape(B, S, SB_WIDTH), sbk.reshape(B, S, SB_WIDTH), sbv.reshape(B, S, SB_WIDTH))
    tqm = 512
    gmax2 = lambda g: jnp.max(jnp.square(g))
    q_bound = jnp.sqrt(MLA_NOPE * gmax2(mla_qn_g) + MLA_ROPE * gmax2(mla_qr_g))
    k_bound = jnp.sqrt(MLA_NOPE * gmax2(mla_kn_g) + MLA_ROPE * gmax2(mla_kr_g))
    logit_bound = 1.02 * q_bound * k_bound * (qk ** -0.5) * LOG2E
    bounded = (logit_bound < MLA_SAFE_LOGIT).astype(jnp.int32).reshape(1)
    mla = pl.pallas_call(
        functools.partial(_mla_attn_kernel, tq=tqm), name="mla_attn",
        grid_spec=pltpu.PrefetchScalarGridSpec(
            num_scalar_prefetch=1, grid=(B, N_PAIRS, S // tqm),
            in_specs=[pl.BlockSpec((1, tqm, PAIR_W), lambda b, p, i, *_: (b, i, p)),
                      pl.BlockSpec((1, S, PAIR_W), lambda b, p, i, *_: (b, 0, p)),
                      pl.BlockSpec((1, S, LANES), lambda b, p, i, *_: (b, 0, p))],
            out_specs=pl.BlockSpec((1, tqm, LANES), lambda b, p, i, *_: (b, i, p)),
            scratch_shapes=[pltpu.VMEM((2 * tqm, 1), F32), pltpu.VMEM((tqm, LANES), F32)]),
        out_shape=jax.ShapeDtypeStruct((B, S, MLA_WIDTH), BF16),
    )(bounded, mq.reshape(B, S, -1), mkk.reshape(B, S, -1), mvv.reshape(B, S, MLA_WIDTH))

    tp = 512
    prow = lambda w: pl.BlockSpec((tp, w), lambda i: (i, 0))
    memspec = pl.BlockSpec((1, M, D), lambda i: ((i * tp) // S, 0, 0))
    W = D + LANES
    h2g, dest, cnt = pl.pallas_call(
        functools.partial(_post_attn_kernel, n_tokens=T), grid=(T // tp,), name="post_attn",
        in_specs=[prow(D), prow(SB_WIDTH), prow(MLA_WIDTH), full((1, SB_WIDTH)), full((1, MLA_WIDTH)),
                  full((D, D)), full((1, D)), full((D, D)), full((1, D // MEM_HEADS)), memspec, memspec,
                  full((D, D)), full((1, D)), full((D, LANES)), full((D, LANES)), full((1, LANES))],
        out_specs=[prow(W), prow(1), pl.BlockSpec((1, 1, LANES), lambda i: (i, 0, 0))],
        out_shape=[jax.ShapeDtypeStruct((T, W), F32), jax.ShapeDtypeStruct((T, 1), jnp.int32),
                   jax.ShapeDtypeStruct((T // tp, 1, LANES), F32)],
        scratch_shapes=[pltpu.VMEM((1, LANES), F32)],
        compiler_params=pltpu.CompilerParams(dimension_semantics=("arbitrary",)),
    )(x.reshape(T, D), sb.reshape(T, SB_WIDTH), mla.reshape(T, MLA_WIDTH), _row(sb_out_g), _row(mla_out_g),
      w_o.astype(BF16), _row(norm_mem_g), w_mq.astype(BF16), _row(mem_qhead_g), mk, mv,
      w_mo.astype(BF16), _row(norm_moe_g), wr_hi, wr_lo, br)

    te = 256
    tg = 2048
    n_buckets = N_GROUPS * PAIRS_PER_GROUP
    n_steps = T // te + n_buckets
    counts = cnt[-1, 0, :n_buckets].astype(jnp.int32)
    tiles = (counts + te - 1) // te
    ends = jnp.cumsum(tiles)
    first_row = (ends - tiles) * te
    n_valid = ends[-1]
    slot = dest.reshape(T)
    dest = first_row[slot // T] + slot % T
    step = jnp.minimum(jnp.arange(n_steps, dtype=jnp.int32), n_valid - 1)
    bucket = jnp.sum(step[:, None] >= ends[None, :], axis=1).astype(jnp.int32)
    pair_a = jnp.array([0, 0, 0, 1, 1, 2], jnp.int32)
    pair_b = jnp.array([1, 2, 3, 2, 3, 3], jnp.int32)
    ea = bucket // PAIRS_PER_GROUP * EXPERTS_PER_GROUP + pair_a[bucket % PAIRS_PER_GROUP]
    eb = bucket // PAIRS_PER_GROUP * EXPERTS_PER_GROUP + pair_b[bucket % PAIRS_PER_GROUP]

    any_spec = pl.BlockSpec(memory_space=pl.ANY)
    h2s = pl.pallas_call(
        _scatter_rows_kernel, name="moe_dispatch",
        grid_spec=pltpu.PrefetchScalarGridSpec(
            num_scalar_prefetch=3, grid=(T // tg,),
            in_specs=[pl.BlockSpec((tg, W), lambda i, *_: (i, 0))],
            out_specs=any_spec,
            scratch_shapes=[pltpu.VMEM((1, W), F32), pltpu.SemaphoreType.DMA, pltpu.SemaphoreType.DMA]),
        out_shape=jax.ShapeDtypeStruct((n_steps * te, W), F32),
        compiler_params=pltpu.CompilerParams(dimension_semantics=("arbitrary",)),
    )(dest, first_row + counts, tiles * te - counts, h2g)

    wspec = lambda shape, which: pl.BlockSpec((1,) + shape, lambda i, blk, ea, eb, nv: ((ea, eb)[which][i], 0, 0))
    w_gate_b, w_up_b, w_down_b = w_gate.astype(BF16), w_up.astype(BF16), w_down.astype(BF16)
    ys = pl.pallas_call(
        _moe_pair_kernel, name="moe_experts",
        grid_spec=pltpu.PrefetchScalarGridSpec(
            num_scalar_prefetch=4, grid=(n_steps,),
            in_specs=[pl.BlockSpec((te, W), lambda i, blk, *_: (blk[i], 0)),
                      pl.BlockSpec((1, D), lambda i, *_: (0, 0)),
                      wspec((D, FF), 0), wspec((D, FF), 0), wspec((FF, D), 0),
                      wspec((D, FF), 1), wspec((D, FF), 1), wspec((FF, D), 1)],
            out_specs=pl.BlockSpec((te, D), lambda i, blk, *_: (blk[i], 0))),
        out_shape=jax.ShapeDtypeStruct((n_steps * te, D), F32),
        compiler_params=pltpu.CompilerParams(dimension_semantics=("arbitrary",)),
    )(step, ea, eb, n_valid.reshape(1), h2s, _row(norm_moe_g),
      w_gate_b, w_up_b, w_down_b, w_gate_b, w_up_b, w_down_b)

    out = pl.pallas_call(
        _gather_rows_kernel, name="moe_combine",
        grid_spec=pltpu.PrefetchScalarGridSpec(
            num_scalar_prefetch=1, grid=(T // tg,),
            in_specs=[any_spec],
            out_specs=pl.BlockSpec((tg, D), lambda i, *_: (i, 0)),
            scratch_shapes=[pltpu.SemaphoreType.DMA]),
        out_shape=jax.ShapeDtypeStruct((T, D), F32),
        compiler_params=pltpu.CompilerParams(dimension_semantics=("arbitrary",)),
    )(dest, ys)
    return out.reshape(B, S, D)
```

```python
import functools

import numpy as np
import jax
import jax.numpy as jnp
from jax import lax
from jax.experimental import pallas as pl
from jax.experimental.pallas import tpu as pltpu

F32 = jnp.float32
BF16 = jnp.bfloat16
EPS = 1e-6
LANES = 128
SUBLANES = 8

CHUNK = 64
SB_HEADS = 8
SB_HEAD_DIM = 64
SB_WIDTH = SB_HEADS * SB_HEAD_DIM
MLA_HEADS = 8
MLA_NOPE = 64
MLA_ROPE = 32
MLA_V = 64
MLA_WIDTH = MLA_HEADS * MLA_V
MLA_Q_RANK = 384
MLA_KV_RANK = 256
ROPE_THETA = 10000.0
MEM_HEADS = 4
N_GROUPS = 4
EXPERTS_PER_GROUP = 4
N_EXPERTS = N_GROUPS * EXPERTS_PER_GROUP
PAIRS_PER_GROUP = EXPERTS_PER_GROUP * (EXPERTS_PER_GROUP - 1) // 2
N_PAIRS = MLA_HEADS // 2
PAIR_W = 2 * LANES
SB_EXP_UNDERFLOW = 104.0
SB_DROPPED = -1e30
LOG2E = 1.4426950408889634
MLA_SAFE_LOGIT = 60.0

_TRANS_B = (((1,), (1,)), ((), ()))


def _rms(x, g):
    return x * lax.rsqrt(jnp.mean(x * x, axis=-1, keepdims=True) + EPS) * g


def _seg_rms(x, masks, seg_len, g):
    x2 = x * x
    r = jnp.zeros_like(x)
    for m in masks:
        s = jnp.sum(jnp.where(m, x2, 0.0), axis=-1, keepdims=True)
        r = jnp.where(m, lax.rsqrt(s * (1.0 / seg_len) + EPS), r)
    return x * r * g


def _mem_kv_kernel(mem_ref, gsrc_ref, w_ref, gk_ref, mk_ref, mv_ref):
    d = mem_ref.shape[-1]
    hd = d // MEM_HEADS
    mn = _rms(mem_ref[0], gsrc_ref[...]).astype(BF16)
    kv = jnp.dot(mn, w_ref[...], preferred_element_type=F32)
    for h in range(MEM_HEADS):
        kh = kv[:, h * hd:(h + 1) * hd]
        mk_ref[0, :, h * hd:(h + 1) * hd] = _rms(kh, gk_ref[...]).astype(BF16)
    mv_ref[0] = kv[:, d:].astype(BF16)


def _in_proj_kernel(x_ref, pos_ref, gmix_ref, w1_ref, gq_ref, wuq_ref, gkv_ref, wukv_ref,
                    gqcat_ref, gkcat_ref, gkr_ref, invf_ref, sgn_ref,
                    sbq_ref, sbk_ref, sbv_ref, mq_ref, mk_ref, mv_ref):
    hn = _rms(x_ref[...], gmix_ref[...]).astype(BF16)
    proj = jnp.dot(hn, w1_ref[...], preferred_element_type=F32)
    sbq_ref[...] = proj[:, 0:SB_WIDTH].astype(BF16)
    sbk_ref[...] = proj[:, SB_WIDTH:2 * SB_WIDTH].astype(BF16)
    sbv_ref[...] = proj[:, 2 * SB_WIDTH:3 * SB_WIDTH].astype(BF16)
    o = 3 * SB_WIDTH
    cq = proj[:, o:o + MLA_Q_RANK]
    ckv = proj[:, o + MLA_Q_RANK:o + MLA_Q_RANK + MLA_KV_RANK]
    kr = proj[:, o + MLA_Q_RANK + MLA_KV_RANK:]

    q = jnp.dot(_rms(cq, gq_ref[...]).astype(BF16), wuq_ref[...], preferred_element_type=F32)
    kv = jnp.dot(_rms(ckv, gkv_ref[...]).astype(BF16), wukv_ref[...], preferred_element_type=F32)

    lane = lax.broadcasted_iota(jnp.int32, (1, LANES), 1)
    lr = lane & 63
    lo, hi = lane < 64, lane >= 64
    r0, r1 = lr < 16, (lr >= 16) & (lr < 32)

    ang = pos_ref[...].astype(F32) * invf_ref[...]
    cos_t = jnp.cos(ang)
    sin_t = jnp.sin(ang) * sgn_ref[...]

    def rope(b):
        return b * cos_t + pltpu.roll(b, 64, 1) * sin_t

    q_scale = (MLA_NOPE + MLA_ROPE) ** -0.5 * LOG2E
    krn = rope(_seg_rms(kr, (r0, r1), MLA_ROPE, gkr_ref[...]))
    for p in range(N_PAIRS):
        b0 = p * PAIR_W
        qa = _seg_rms(q[:, b0:b0 + LANES], (lo, hi), MLA_NOPE, gqcat_ref[:, b0:b0 + LANES])
        qb = rope(_seg_rms(q[:, b0 + LANES:b0 + PAIR_W], (r0, r1), MLA_ROPE,
                           gqcat_ref[:, b0 + LANES:b0 + PAIR_W]))
        mq_ref[:, b0:b0 + LANES] = (qa * q_scale).astype(BF16)
        mq_ref[:, b0 + LANES:b0 + PAIR_W] = (qb * q_scale).astype(BF16)
        ka = _seg_rms(kv[:, b0:b0 + LANES], (lo, hi), MLA_NOPE, gkcat_ref[:, b0:b0 + LANES])
        mk_ref[:, b0:b0 + LANES] = ka.astype(BF16)
        mk_ref[:, b0 + LANES:b0 + PAIR_W] = krn.astype(BF16)
    mv_ref[...] = kv[:, N_PAIRS * PAIR_W:].T.astype(BF16)


def _sb_attn_kernel(q_ref, k_ref, v_ref, o_ref, *, tq):
    i = pl.program_id(2)
    q = q_ref[0]
    lane = lax.broadcasted_iota(jnp.int32, (1, LANES), 1)
    lo = lane < 64
    zq = jnp.zeros_like(q)
    qcat = jnp.concatenate([jnp.where(lo, q, zq), jnp.where(lo, zq, q)], axis=0)
    row = lax.broadcasted_iota(jnp.int32, (tq, tq), 0)
    col = lax.broadcasted_iota(jnp.int32, (tq, tq), 1)
    later = (row > col).astype(BF16)
    past = jnp.concatenate([col < row, col < row], axis=0)

    def scores(j, keep=None):
        start = pl.multiple_of(j * tq, tq)
        z = lax.dot_general(qcat, k_ref[0, pl.ds(start, tq), :], _TRANS_B, preferred_element_type=F32)
        if keep is not None:
            z = jnp.where(keep, z, SB_DROPPED)
        return z, jnp.maximum(z, 0.0) + jnp.log(1.0 + jnp.exp(-jnp.abs(z)))

    def weights(z, fail, after):
        between = after + jnp.dot(fail.astype(BF16), later, preferred_element_type=F32)
        return jnp.exp((z - fail) - between).astype(BF16)

    def pv(ws, js):
        zv = jnp.zeros((tq, LANES), BF16)
        vs = [v_ref[0, pl.ds(pl.multiple_of(j * tq, tq), tq), :] for j in js]
        wcat = jnp.concatenate([w[:tq] for w in ws] + [w[tq:] for w in ws], axis=1)
        vcat = jnp.concatenate([jnp.where(lo, v, zv) for v in vs] + [jnp.where(lo, zv, v) for v in vs], axis=0)
        return jnp.dot(wcat, vcat, preferred_element_type=F32)

    prev = jnp.maximum(i - 1, 0)
    zb, fb = scores(i, past)
    tot_b = jnp.sum(fb, axis=1, keepdims=True)
    wb = weights(zb, fb, 0.0)
    za, fa = scores(prev, i > 0)
    wa = weights(za, fa, tot_b)
    carry0 = tot_b + jnp.sum(fa, axis=1, keepdims=True)
    acc0 = pv([wa, wb], [prev, i])

    def cond(st):
        j, m, _, _ = st
        return (j >= 0) & (m < SB_EXP_UNDERFLOW)

    def body(st):
        j, _, carry, acc = st
        z, fail = scores(j)
        w = weights(z, fail, carry)
        carry = carry + jnp.sum(fail, axis=1, keepdims=True)
        return j - 1, jnp.min(carry), carry, acc + pv([w], [j])

    _, _, _, acc = lax.while_loop(cond, body, (i - 2, jnp.min(carry0), carry0, acc0))
    o_ref[0] = acc.astype(o_ref.dtype)


def _mla_attn_kernel(bounded_ref, q_ref, k_ref, v_ref, o_ref, l_ref, acc_ref, *, tq):
    i = pl.program_id(2)
    q = q_ref[0]
    l2 = lax.broadcasted_iota(jnp.int32, (1, PAIR_W), 1)
    lr = l2 & 63
    hm0 = (l2 < 64) | ((l2 >= LANES) & (lr < 16))
    hm1 = ((l2 >= 64) & (l2 < LANES)) | ((l2 >= LANES) & (lr >= 16) & (lr < 32))
    zq = jnp.zeros_like(q)
    qcat = jnp.concatenate([jnp.where(hm0, q, zq), jnp.where(hm1, q, zq)], axis=0)
    key = lax.broadcasted_iota(jnp.int32, (tq, 2 * tq), 0)
    qry = lax.broadcasted_iota(jnp.int32, (tq, 2 * tq), 1) & (tq - 1)
    visible = (key // CHUNK) <= (qry // CHUNK)
    head0 = lax.broadcasted_iota(jnp.int32, (LANES, 1), 0) < 64

    def per_head(x):
        return jnp.where(head0, x[:, :tq], x[:, tq:])

    def logits(j, n, diag):
        start = pl.multiple_of(j * tq, tq)
        s = lax.dot_general(k_ref[0, pl.ds(start, n * tq), :], qcat, _TRANS_B, preferred_element_type=F32)
        if diag:
            vis = visible if n == 1 else jnp.concatenate([jnp.ones_like(visible)] * (n - 1) + [visible], axis=0)
            s = jnp.where(vis, s, -jnp.inf)
        return s

    def pv(j, n, p):
        vt = v_ref[:, pl.ds(pl.multiple_of(j * tq, tq), n * tq)]
        pb = p.astype(BF16)
        zv = jnp.zeros_like(vt)
        return jnp.dot(jnp.concatenate([jnp.where(head0, vt, zv), jnp.where(head0, zv, vt)], axis=1),
                       jnp.concatenate([pb[:, :tq], pb[:, tq:]], axis=0),
                       preferred_element_type=F32)

    def finish(l, acc):
        o_ref[0] = (acc / per_head(l)).T.astype(o_ref.dtype)

    @pl.when(bounded_ref[0] != 0)
    def _():
        def step(j, n, diag):
            p = jnp.exp2(logits(j, n, diag))
            l_ref[...] += jnp.sum(p, axis=0, keepdims=True)
            acc_ref[...] += pv(j, n, p)

        l_ref[...] = jnp.zeros_like(l_ref)
        acc_ref[...] = jnp.zeros_like(acc_ref)

        @pl.loop(0, i // 2)
        def _(t):
            step(2 * t, 2, False)

        @pl.when(i % 2 == 1)
        def _():
            step(i - 1, 2, True)

        @pl.when(i % 2 == 0)
        def _():
            step(i, 1, True)

        finish(l_ref[...], acc_ref[...])

    @pl.when(bounded_ref[0] == 0)
    def _():
        def step(j, st, diag):
            m, l, acc = st
            s = logits(j, 1, diag)
            m_new = jnp.maximum(m, jnp.max(s, axis=0, keepdims=True))
            alpha = jnp.exp2(m - m_new)
            p = jnp.exp2(s - m_new)
            l_new = alpha * l + jnp.sum(p, axis=0, keepdims=True)
            return m_new, l_new, per_head(alpha) * acc + pv(j, 1, p)

        st = step(i, (jnp.full((1, 2 * tq), -jnp.inf, F32), jnp.zeros((1, 2 * tq), F32),
                      jnp.zeros((LANES, tq), F32)), True)
        _, l, acc = lax.fori_loop(0, i, lambda j, st: step(j, st, False), st)
        finish(l, acc)


def _post_attn_kernel(x_ref, sb_ref, mla_ref, gsb_ref, gmla_ref, wo_ref, gmem_ref, wmq_ref, gqh_ref,
                      mk_ref, mv_ref, wmo_ref, gmoe_ref, wr_hi_ref, wr_lo_ref, br_ref,
                      h2g_ref, dest_ref, cnt_ref, carry_ref, *, n_tokens):
    d = x_ref.shape[-1]
    hd = d // MEM_HEADS
    sbn = _rms(sb_ref[...].astype(F32), gsb_ref[...]).astype(BF16)
    mlan = _rms(mla_ref[...].astype(F32), gmla_ref[...]).astype(BF16)
    mixed = jnp.concatenate([sbn, mlan], axis=1)
    h1 = x_ref[...] + jnp.dot(mixed, wo_ref[...], preferred_element_type=F32)

    hq = _rms(h1, gmem_ref[...]).astype(BF16)
    mq = jnp.dot(hq, wmq_ref[...], preferred_element_type=F32)
    mos = []
    for h in range(MEM_HEADS):
        qh = (_rms(mq[:, h * hd:(h + 1) * hd], gqh_ref[...]) * (hd ** -0.5)).astype(BF16)
        sc = lax.dot_general(qh, mk_ref[0, :, h * hd:(h + 1) * hd], _TRANS_B, preferred_element_type=F32)
        sc = sc - jnp.max(sc, axis=-1, keepdims=True)
        e = jnp.exp(sc)
        p = (e / jnp.sum(e, axis=-1, keepdims=True)).astype(BF16)
        mos.append(jnp.dot(p, mv_ref[0, :, h * hd:(h + 1) * hd], preferred_element_type=F32).astype(BF16))
    h2 = h1 + jnp.dot(jnp.concatenate(mos, axis=1), wmo_ref[...], preferred_element_type=F32)
    h2g_ref[:, 0:d] = h2

    t = _rms(h2, gmoe_ref[...])
    t_hi = t.astype(BF16)
    t_lo = (t - t_hi.astype(F32)).astype(BF16)
    logits = (jnp.dot(t_hi, wr_hi_ref[...], preferred_element_type=F32)
              + jnp.dot(t_hi, wr_lo_ref[...], preferred_element_type=F32)
              + jnp.dot(t_lo, wr_hi_ref[...], preferred_element_type=F32)) + br_ref[...]

    lane = lax.broadcasted_iota(jnp.int32, (1, LANES), 1).astype(F32)
    big = float(LANES)
    neg = -jnp.inf
    lg = jnp.where(lane < N_GROUPS, logits, neg)
    gmax = jnp.max(lg, axis=-1, keepdims=True)
    g_idx = jnp.min(jnp.where(lg == gmax, lane, big), axis=-1, keepdims=True)
    g_w = 1.0 / jnp.sum(jnp.exp(lg - gmax), axis=-1, keepdims=True)
    e_lo = N_GROUPS + g_idx * EXPERTS_PER_GROUP
    in_grp = (lane >= e_lo) & (lane < e_lo + EXPERTS_PER_GROUP)
    le = jnp.where(in_grp, logits, neg)
    m1 = jnp.max(le, axis=-1, keepdims=True)
    i1 = jnp.min(jnp.where(le == m1, lane, big), axis=-1, keepdims=True)
    le2 = jnp.where(lane == i1, neg, le)
    m2 = jnp.max(le2, axis=-1, keepdims=True)
    i2 = jnp.min(jnp.where(le2 == m2, lane, big), axis=-1, keepdims=True)
    e2 = jnp.exp(m2 - m1)
    w1 = g_w / (1.0 + e2)
    w2 = g_w * e2 / (1.0 + e2)
    h2g_ref[:, d:] = jnp.where(lane == i1, w1, 0.0) + jnp.where(lane == i2, w2, 0.0)

    @pl.when(pl.program_id(0) == 0)
    def _():
        carry_ref[...] = jnp.zeros_like(carry_ref)

    ka = jnp.minimum(i1, i2) - e_lo
    kb = jnp.maximum(i1, i2) - e_lo
    pair = ka * (5.0 - ka) * 0.5 + kb - 1.0
    bucket = g_idx * float(PAIRS_PER_GROUP) + pair
    tp = x_ref.shape[0]
    oh = jnp.where(lane == bucket, 1.0, 0.0)
    earlier = (lax.broadcasted_iota(jnp.int32, (tp, tp), 1)
               < lax.broadcasted_iota(jnp.int32, (tp, tp), 0)).astype(BF16)
    prefix = jnp.dot(earlier, oh.astype(BF16), preferred_element_type=F32)
    carry = carry_ref[...]
    rank = jnp.sum(jnp.where(lane == bucket, prefix + carry, 0.0), axis=-1, keepdims=True)
    dest_ref[...] = (bucket * float(n_tokens) + rank).astype(jnp.int32)
    carry_ref[...] = carry + jnp.sum(oh, axis=0, keepdims=True)
    cnt_ref[0] = carry_ref[...]


def _scatter_rows_kernel(dest_ref, fill_start_ref, fill_n_ref, x_ref, o_hbm, zrow_ref, sem, zsem):
    i = pl.program_id(0)
    tg = x_ref.shape[0]
    base = i * tg

    @pl.loop(0, tg // SUBLANES)
    def _(r8):
        r0 = pl.multiple_of(r8 * SUBLANES, SUBLANES)
        for k in range(SUBLANES):
            pltpu.make_async_copy(x_ref.at[pl.ds(r0 + k, 1)],
                                  o_hbm.at[pl.ds(dest_ref[base + r0 + k], 1)], sem).start()

    pltpu.make_async_copy(x_ref, o_hbm.at[pl.ds(0, tg)], sem).wait()

    @pl.when(i == pl.num_programs(0) - 1)
    def _():
        zrow_ref[...] = jnp.zeros_like(zrow_ref)

        def fill(b, act):
            n = fill_n_ref[b]
            pos = fill_start_ref[b]
            head = jnp.minimum((-pos) & (SUBLANES - 1), n)
            zcopy = lambda at, size: pltpu.make_async_copy(zrow_ref.at[pl.ds(0, size)],
                                                           o_hbm.at[pl.ds(at, size)], zsem)
            for k in range(SUBLANES - 1):
                pl.when(k < head)(functools.partial(act, zcopy(pos + k, 1)))
            pos = pos + head
            n = n - head
            size = zrow_ref.shape[0]
            while size >= SUBLANES:
                pl.when((n & size) != 0)(functools.partial(act, zcopy(pl.multiple_of(pos, SUBLANES), size)))
                pos = pos + (n & size)
                size //= 2

        pl.loop(0, fill_n_ref.shape[0])(lambda b: fill(b, lambda c: c.start()))
        pl.loop(0, fill_n_ref.shape[0])(lambda b: fill(b, lambda c: c.wait()))


def _moe_pair_kernel(blk_ref, ea_ref, eb_ref, nvalid_ref, x_ref, gmoe_ref,
                     wga_ref, wua_ref, wda_ref, wgb_ref, wub_ref, wdb_ref, o_ref):
    i = pl.program_id(0)
    d = o_ref.shape[-1]

    @pl.when(i < nvalid_ref[0])
    def _():
        x = x_ref[...]
        h2 = x[:, 0:d]
        gates = x[:, d:]
        t = _rms(h2, gmoe_ref[...]).astype(BF16)
        lane = lax.broadcasted_iota(jnp.int32, (1, LANES), 1)

        def hidden(e, wg_ref, wu_ref):
            ge = jnp.sum(jnp.where(lane == N_GROUPS + e, gates, 0.0), axis=-1, keepdims=True)
            a = jnp.dot(t, wg_ref[0], preferred_element_type=F32)
            u = jnp.dot(t, wu_ref[0], preferred_element_type=F32)
            return (a * jax.nn.sigmoid(a) * u * ge).astype(BF16)

        o_ref[...] = (h2 + jnp.dot(hidden(ea_ref[i], wga_ref, wua_ref), wda_ref[0], preferred_element_type=F32)
                      + jnp.dot(hidden(eb_ref[i], wgb_ref, wub_ref), wdb_ref[0], preferred_element_type=F32))


def _gather_rows_kernel(dest_ref, y_hbm, o_ref, sem):
    tg = o_ref.shape[0]
    base = pl.program_id(0) * tg

    @pl.loop(0, tg // SUBLANES)
    def _(r8):
        r0 = pl.multiple_of(r8 * SUBLANES, SUBLANES)
        for k in range(SUBLANES):
            pltpu.make_async_copy(y_hbm.at[pl.ds(dest_ref[base + r0 + k], 1)],
                                  o_ref.at[pl.ds(r0 + k, 1)], sem).start()

    pltpu.make_async_copy(y_hbm.at[pl.ds(0, tg)], o_ref, sem).wait()


def _rope_block(x1, x2):
    z = jnp.zeros(x1.shape[:-1] + (32,), x1.dtype)
    return jnp.concatenate([x1, x1, z, x2, x2, z], axis=-1)


def _pair_rope_block(a, b):
    z = jnp.zeros(a.shape[:-1] + (32,), a.dtype)
    return jnp.concatenate([a[..., :16], b[..., :16], z, a[..., 16:], b[..., 16:], z], axis=-1)


def _row(v):
    return v.reshape(1, -1).astype(F32)


def kernel(x, mem, positions, norm_mix_g, w_in, mla_q_norm_g, w_uq, mla_kv_norm_g, w_ukv, mla_qn_g, mla_qr_g, mla_kn_g, mla_kr_g, sb_out_g, mla_out_g, w_o, norm_mem_g, mem_src_g, w_mq, w_mkv, mem_qhead_g, mem_khead_g, w_mo, norm_moe_g, w_group, b_group, w_router, b_router, w_gate, w_up, w_down):
    B, S, D = x.shape
    T = B * S
    M = mem.shape[1]
    FF = w_gate.shape[-1]
    half = MLA_ROPE // 2
    qk = MLA_NOPE + MLA_ROPE

    o = 3 * SB_WIDTH + MLA_Q_RANK + MLA_KV_RANK
    w1 = jnp.concatenate([w_in[:, :SB_WIDTH] * (SB_HEAD_DIM ** -0.5), w_in[:, SB_WIDTH:o],
                          _rope_block(w_in[:, o:o + half], w_in[:, o + half:])], axis=1).astype(BF16)
    uq = w_uq.reshape(MLA_Q_RANK, MLA_HEADS, qk)
    ukv = w_ukv.reshape(MLA_KV_RANK, MLA_HEADS, MLA_NOPE + MLA_V)
    zk = jnp.zeros((MLA_KV_RANK, LANES), F32)
    uq_cols, uk_cols = [], []
    for p in range(N_PAIRS):
        a, b = 2 * p, 2 * p + 1
        uq_cols += [uq[:, a, :MLA_NOPE], uq[:, b, :MLA_NOPE], _pair_rope_block(uq[:, a, MLA_NOPE:], uq[:, b, MLA_NOPE:])]
        uk_cols += [ukv[:, a, :MLA_NOPE], ukv[:, b, :MLA_NOPE], zk]
    wuq = jnp.concatenate(uq_cols, axis=1).astype(BF16)
    wukv = jnp.concatenate(uk_cols + [ukv[:, h, MLA_NOPE:] for h in range(MLA_HEADS)], axis=1).astype(BF16)
    gq_pair = jnp.concatenate([mla_qn_g, mla_qn_g, _pair_rope_block(mla_qr_g, mla_qr_g)])
    gk_pair = jnp.concatenate([mla_kn_g, mla_kn_g, jnp.zeros((LANES,), F32)])
    gqcat = _row(jnp.tile(gq_pair, N_PAIRS))
    gkcat = _row(jnp.tile(gk_pair, N_PAIRS))
    gkr = _row(_rope_block(mla_kr_g[:half], mla_kr_g[half:]))
    inv_freq = ROPE_THETA ** (-(jnp.arange(half, dtype=F32) * 2.0 / MLA_ROPE))
    invf = _row(_rope_block(inv_freq, inv_freq))
    sgn = _row(jnp.concatenate([-jnp.ones((64,), F32), jnp.ones((64,), F32)]))
    wr = jnp.concatenate([w_group, w_router, jnp.zeros((D, LANES - N_GROUPS - N_EXPERTS), F32)], axis=1)
    wr_hi = wr.astype(BF16)
    wr_lo = (wr - wr_hi.astype(F32)).astype(BF16)
    br = _row(jnp.concatenate([b_group, b_router, jnp.zeros((LANES - N_GROUPS - N_EXPERTS,), F32)]))

    full = lambda shape: pl.BlockSpec(shape, lambda *_: (0,) * len(shape))

    mk, mv = pl.pallas_call(
        _mem_kv_kernel, grid=(B,), name="mem_kv",
        in_specs=[pl.BlockSpec((1, M, D), lambda b: (b, 0, 0)), full((1, D)), full((D, 2 * D)),
                  full((1, D // MEM_HEADS))],
        out_specs=[pl.BlockSpec((1, M, D), lambda b: (b, 0, 0))] * 2,
        out_shape=[jax.ShapeDtypeStruct((B, M, D), BF16)] * 2,
    )(mem, _row(mem_src_g), w_mkv.astype(BF16), _row(mem_khead_g))

    tm = 256
    rows = lambda w: pl.BlockSpec((tm, w), lambda i: (i, 0))
    sbq, sbk, sbv, mq, mkk, mvv = pl.pallas_call(
        _in_proj_kernel, grid=(T // tm,), name="in_proj",
        in_specs=[rows(D), rows(1), full((1, D)), full(w1.shape), full((1, MLA_Q_RANK)), full(wuq.shape),
                  full((1, MLA_KV_RANK)), full(wukv.shape), full(gqcat.shape), full(gkcat.shape),
                  full((1, LANES)), full((1, LANES)), full((1, LANES))],
        out_specs=[rows(SB_WIDTH)] * 3 + [rows(N_PAIRS * PAIR_W)] * 2
                  + [pl.BlockSpec((MLA_WIDTH, tm), lambda i: (0, i))],
        out_shape=[jax.ShapeDtypeStruct((T, SB_WIDTH), BF16)] * 3
                  + [jax.ShapeDtypeStruct((T, N_PAIRS * PAIR_W), BF16)] * 2
                  + [jax.ShapeDtypeStruct((MLA_WIDTH, T), BF16)],
    )(x.reshape(T, D), positions.reshape(T, 1), _row(norm_mix_g), w1, _row(mla_q_norm_g), wuq,
      _row(mla_kv_norm_g), wukv, gqcat, gkcat, gkr, invf, sgn)

    qspec = lambda t, w: pl.BlockSpec((1, t, w), lambda b, p, i: (b, i, p))
    kvspec = lambda w: pl.BlockSpec((1, S, w), lambda b, p, i: (b, 0, p))
    tq = 256
    sb = pl.pallas_call(
        functools.partial(_sb_attn_kernel, tq=tq), grid=(B, SB_HEADS // 2, S // tq), name="sb_attn",
        in_specs=[qspec(tq, LANES), kvspec(LANES), kvspec(LANES)],
        out_specs=qspec(tq, LANES),
        out_shape=jax.ShapeDtypeStruct((B, S, SB_WIDTH), BF16),
    )(sbq.reshape(B, S, SB_WIDTH), sbk.reshape(B, S, SB_WIDTH), sbv.reshape(B, S, SB_WIDTH))
    tqm = 512
    gmax2 = lambda g: jnp.max(jnp.square(g))
    q_bound = jnp.sqrt(MLA_NOPE * gmax2(mla_qn_g) + MLA_ROPE * gmax2(mla_qr_g))
    k_bound = jnp.sqrt(MLA_NOPE * gmax2(mla_kn_g) + MLA_ROPE * gmax2(mla_kr_g))
    logit_bound = 1.02 * q_bound * k_bound * (qk ** -0.5) * LOG2E
    bounded = (logit_bound < MLA_SAFE_LOGIT).astype(jnp.int32).reshape(1)
    mla = pl.pallas_call(
        functools.partial(_mla_attn_kernel, tq=tqm), name="mla_attn",
        grid_spec=pltpu.PrefetchScalarGridSpec(
            num_scalar_prefetch=1, grid=(B, N_PAIRS, S // tqm),
            in_specs=[pl.BlockSpec((1, tqm, PAIR_W), lambda b, p, i, *_: (b, i, p)),
                      pl.BlockSpec((1, S, PAIR_W), lambda b, p, i, *_: (b, 0, p)),
                      pl.BlockSpec((LANES, S), lambda b, p, i, *_: (p, b))],
            out_specs=pl.BlockSpec((1, tqm, LANES), lambda b, p, i, *_: (b, i, p)),
            scratch_shapes=[pltpu.VMEM((1, 2 * tqm), F32), pltpu.VMEM((LANES, tqm), F32)]),
        out_shape=jax.ShapeDtypeStruct((B, S, MLA_WIDTH), BF16),
    )(bounded, mq.reshape(B, S, -1), mkk.reshape(B, S, -1), mvv)

    tp = 512
    prow = lambda w: pl.BlockSpec((tp, w), lambda i: (i, 0))
    memspec = pl.BlockSpec((1, M, D), lambda i: ((i * tp) // S, 0, 0))
    W = D + LANES
    h2g, dest, cnt = pl.pallas_call(
        functools.partial(_post_attn_kernel, n_tokens=T), grid=(T // tp,), name="post_attn",
        in_specs=[prow(D), prow(SB_WIDTH), prow(MLA_WIDTH), full((1, SB_WIDTH)), full((1, MLA_WIDTH)),
                  full((D, D)), full((1, D)), full((D, D)), full((1, D // MEM_HEADS)), memspec, memspec,
                  full((D, D)), full((1, D)), full((D, LANES)), full((D, LANES)), full((1, LANES))],
        out_specs=[prow(W), prow(1), pl.BlockSpec((1, 1, LANES), lambda i: (i, 0, 0))],
        out_shape=[jax.ShapeDtypeStruct((T, W), F32), jax.ShapeDtypeStruct((T, 1), jnp.int32),
                   jax.ShapeDtypeStruct((T // tp, 1, LANES), F32)],
        scratch_shapes=[pltpu.VMEM((1, LANES), F32)],
        compiler_params=pltpu.CompilerParams(dimension_semantics=("arbitrary",)),
    )(x.reshape(T, D), sb.reshape(T, SB_WIDTH), mla.reshape(T, MLA_WIDTH), _row(sb_out_g), _row(mla_out_g),
      w_o.astype(BF16), _row(norm_mem_g), w_mq.astype(BF16), _row(mem_qhead_g), mk, mv,
      w_mo.astype(BF16), _row(norm_moe_g), wr_hi, wr_lo, br)

    te = 256
    tg = 2048
    n_buckets = N_GROUPS * PAIRS_PER_GROUP
    n_steps = T // te + n_buckets
    counts = cnt[-1, 0, :n_buckets].astype(jnp.int32)
    tiles = (counts + te - 1) // te
    ends = jnp.cumsum(tiles)
    first_row = (ends - tiles) * te
    n_valid = ends[-1]
    slot = dest.reshape(T // LANES, LANES)
    slot_bucket = slot // T
    row0 = jnp.zeros_like(slot)
    for b in range(n_buckets):
        row0 = jnp.where(slot_bucket == b, first_row[b], row0)
    dest = (row0 + slot % T).reshape(T)
    step = jnp.minimum(jnp.arange(n_steps, dtype=jnp.int32), n_valid - 1)
    bucket = jnp.zeros_like(step)
    for b in range(n_buckets):
        bucket = bucket + (step >= ends[b]).astype(jnp.int32)
    pr = bucket % PAIRS_PER_GROUP
    pair_a = (pr >= 3).astype(jnp.int32) + (pr >= 5).astype(jnp.int32)
    pair_b = pr + 1 - 2 * (pr >= 3).astype(jnp.int32) - (pr >= 5).astype(jnp.int32)
    ea = bucket // PAIRS_PER_GROUP * EXPERTS_PER_GROUP + pair_a
    eb = bucket // PAIRS_PER_GROUP * EXPERTS_PER_GROUP + pair_b

    any_spec = pl.BlockSpec(memory_space=pl.ANY)
    h2s = pl.pallas_call(
        _scatter_rows_kernel, name="moe_dispatch",
        grid_spec=pltpu.PrefetchScalarGridSpec(
            num_scalar_prefetch=3, grid=(T // tg,),
            in_specs=[pl.BlockSpec((tg, W), lambda i, *_: (i, 0))],
            out_specs=any_spec,
            scratch_shapes=[pltpu.VMEM((te // 2, W), F32),
                            pltpu.SemaphoreType.DMA, pltpu.SemaphoreType.DMA]),
        out_shape=jax.ShapeDtypeStruct((n_steps * te, W), F32),
        compiler_params=pltpu.CompilerParams(dimension_semantics=("arbitrary",)),
    )(dest, first_row + counts, tiles * te - counts, h2g)

    wspec = lambda shape, which: pl.BlockSpec((1,) + shape, lambda i, blk, ea, eb, nv: ((ea, eb)[which][i], 0, 0))
    w_gate_b, w_up_b, w_down_b = w_gate.astype(BF16), w_up.astype(BF16), w_down.astype(BF16)
    ys = pl.pallas_call(
        _moe_pair_kernel, name="moe_experts",
        grid_spec=pltpu.PrefetchScalarGridSpec(
            num_scalar_prefetch=4, grid=(n_steps,),
            in_specs=[pl.BlockSpec((te, W), lambda i, blk, *_: (blk[i], 0)),
                      pl.BlockSpec((1, D), lambda i, *_: (0, 0)),
                      wspec((D, FF), 0), wspec((D, FF), 0), wspec((FF, D), 0),
                      wspec((D, FF), 1), wspec((D, FF), 1), wspec((FF, D), 1)],
            out_specs=pl.BlockSpec((te, D), lambda i, blk, *_: (blk[i], 0))),
        out_shape=jax.ShapeDtypeStruct((n_steps * te, D), F32),
        compiler_params=pltpu.CompilerParams(dimension_semantics=("arbitrary",)),
    )(step, ea, eb, n_valid.reshape(1), h2s, _row(norm_moe_g),
      w_gate_b, w_up_b, w_down_b, w_gate_b, w_up_b, w_down_b)

    out = pl.pallas_call(
        _gather_rows_kernel, name="moe_combine",
        grid_spec=pltpu.PrefetchScalarGridSpec(
            num_scalar_prefetch=1, grid=(T // tg,),
            in_specs=[any_spec],
            out_specs=pl.BlockSpec((tg, D), lambda i, *_: (i, 0)),
            scratch_shapes=[pltpu.SemaphoreType.DMA]),
        out_shape=jax.ShapeDtypeStruct((T, D), F32),
        compiler_params=pltpu.CompilerParams(dimension_semantics=("arbitrary",)),
    )(dest, ys)
    return out.reshape(B, S, D)
```

```python
import functools

import numpy as np
import jax
import jax.numpy as jnp
from jax import lax
from jax.experimental import pallas as pl
from jax.experimental.pallas import tpu as pltpu

F32 = jnp.float32
BF16 = jnp.bfloat16
EPS = 1e-6
LANES = 128
SUBLANES = 8

CHUNK = 64
SB_HEADS = 8
SB_HEAD_DIM = 64
SB_WIDTH = SB_HEADS * SB_HEAD_DIM
MLA_HEADS = 8
MLA_NOPE = 64
MLA_ROPE = 32
MLA_V = 64
MLA_WIDTH = MLA_HEADS * MLA_V
MLA_Q_RANK = 384
MLA_KV_RANK = 256
ROPE_THETA = 10000.0
MEM_HEADS = 4
N_GROUPS = 4
EXPERTS_PER_GROUP = 4
N_EXPERTS = N_GROUPS * EXPERTS_PER_GROUP
PAIRS_PER_GROUP = EXPERTS_PER_GROUP * (EXPERTS_PER_GROUP - 1) // 2
POST_ATTN_CHUNKS = 2
N_PAIRS = MLA_HEADS // 2
PAIR_W = 2 * LANES
SB_EXP_UNDERFLOW = 104.0
SB_DROPPED = -1e30
LOG2E = 1.4426950408889634
MLA_SAFE_LOGIT = 60.0

_TRANS_B = (((1,), (1,)), ((), ()))


def _rms(x, g):
    return x * lax.rsqrt(jnp.mean(x * x, axis=-1, keepdims=True) + EPS) * g


def _seg_rms(x, masks, seg_len, g):
    x2 = x * x
    r = jnp.zeros_like(x)
    for m in masks:
        s = jnp.sum(jnp.where(m, x2, 0.0), axis=-1, keepdims=True)
        r = jnp.where(m, lax.rsqrt(s * (1.0 / seg_len) + EPS), r)
    return x * r * g


def _mem_kv_kernel(mem_ref, gsrc_ref, w_ref, gk_ref, mk_ref, mv_ref):
    d = mem_ref.shape[-1]
    hd = d // MEM_HEADS
    mn = _rms(mem_ref[0], gsrc_ref[...]).astype(BF16)
    kv = jnp.dot(mn, w_ref[...], preferred_element_type=F32)
    for h in range(MEM_HEADS):
        kh = kv[:, h * hd:(h + 1) * hd]
        mk_ref[0, :, h * hd:(h + 1) * hd] = _rms(kh, gk_ref[...]).astype(BF16)
    mv_ref[0] = kv[:, d:].astype(BF16)


def _in_proj_kernel(x_ref, pos_ref, gmix_ref, wlat_ref, wsb_ref, gq_ref, wuq_ref, gkv_ref, wukv_ref,
                    gqcat_ref, gkcat_ref, gkr_ref, invf_ref, sgn_ref,
                    sbq_ref, sbk_ref, sbv_ref, mq_ref, mk_ref, mv_ref, cos_ref, sin_ref):
    ang = pos_ref[...].astype(F32) * invf_ref[...]
    cos_ref[...] = jnp.cos(ang)
    sin_ref[...] = jnp.sin(ang) * sgn_ref[...]

    hn = _rms(x_ref[...], gmix_ref[...]).astype(BF16)
    lat = jnp.dot(hn, wlat_ref[...], preferred_element_type=F32)
    cq = lat[:, 0:MLA_Q_RANK]
    ckv = lat[:, MLA_Q_RANK:MLA_Q_RANK + MLA_KV_RANK]
    kr = lat[:, MLA_Q_RANK + MLA_KV_RANK:]
    q = jnp.dot(_rms(cq, gq_ref[...]).astype(BF16), wuq_ref[...], preferred_element_type=F32)
    kv = jnp.dot(_rms(ckv, gkv_ref[...]).astype(BF16), wukv_ref[...], preferred_element_type=F32)

    sb = jnp.dot(hn, wsb_ref[...], preferred_element_type=F32)
    sbq_ref[...] = sb[:, 0:SB_WIDTH].astype(BF16)
    sbk_ref[...] = sb[:, SB_WIDTH:2 * SB_WIDTH].astype(BF16)
    sbv_ref[...] = sb[:, 2 * SB_WIDTH:3 * SB_WIDTH].astype(BF16)

    lane = lax.broadcasted_iota(jnp.int32, (1, LANES), 1)
    lr = lane & 63
    lo, hi = lane < 64, lane >= 64
    r0, r1 = lr < 16, (lr >= 16) & (lr < 32)

    def rope(b):
        return b * cos_ref[...] + pltpu.roll(b, 64, 1) * sin_ref[...]

    q_scale = (MLA_NOPE + MLA_ROPE) ** -0.5 * LOG2E
    krn = rope(_seg_rms(kr, (r0, r1), MLA_ROPE, gkr_ref[...]))
    for p in range(N_PAIRS):
        b0 = p * PAIR_W
        qa = _seg_rms(q[:, b0:b0 + LANES], (lo, hi), MLA_NOPE, gqcat_ref[:, b0:b0 + LANES])
        qb = rope(_seg_rms(q[:, b0 + LANES:b0 + PAIR_W], (r0, r1), MLA_ROPE,
                           gqcat_ref[:, b0 + LANES:b0 + PAIR_W]))
        mq_ref[:, b0:b0 + LANES] = (qa * q_scale).astype(BF16)
        mq_ref[:, b0 + LANES:b0 + PAIR_W] = (qb * q_scale).astype(BF16)
        ka = _seg_rms(kv[:, b0:b0 + LANES], (lo, hi), MLA_NOPE, gkcat_ref[:, b0:b0 + LANES])
        mk_ref[:, b0:b0 + LANES] = ka.astype(BF16)
        mk_ref[:, b0 + LANES:b0 + PAIR_W] = krn.astype(BF16)
    mv_ref[...] = kv[:, N_PAIRS * PAIR_W:].T.astype(BF16)


def _sb_attn_kernel(q_ref, k_ref, v_ref, o_ref, *, tq):
    i = pl.program_id(2)
    q = q_ref[0]
    lane = lax.broadcasted_iota(jnp.int32, (1, LANES), 1)
    lo = lane < 64
    zq = jnp.zeros_like(q)
    qcat = jnp.concatenate([jnp.where(lo, q, zq), jnp.where(lo, zq, q)], axis=0)
    row = lax.broadcasted_iota(jnp.int32, (tq, tq), 0)
    col = lax.broadcasted_iota(jnp.int32, (tq, tq), 1)
    later = (row > col).astype(BF16)
    past = jnp.concatenate([col < row, col < row], axis=0)

    def scores(j, keep=None):
        start = pl.multiple_of(j * tq, tq)
        z = lax.dot_general(qcat, k_ref[0, pl.ds(start, tq), :], _TRANS_B, preferred_element_type=F32)
        if keep is not None:
            z = jnp.where(keep, z, SB_DROPPED)
        return z, jnp.maximum(z, 0.0) + jnp.log(1.0 + jnp.exp(-jnp.abs(z)))

    def weights(z, fail, after):
        between = after + jnp.dot(fail.astype(BF16), later, preferred_element_type=F32)
        return jnp.exp((z - fail) - between).astype(BF16)

    def pv(ws, js):
        zv = jnp.zeros((tq, LANES), BF16)
        vs = [v_ref[0, pl.ds(pl.multiple_of(j * tq, tq), tq), :] for j in js]
        wcat = jnp.concatenate([w[:tq] for w in ws] + [w[tq:] for w in ws], axis=1)
        vcat = jnp.concatenate([jnp.where(lo, v, zv) for v in vs] + [jnp.where(lo, zv, v) for v in vs], axis=0)
        return jnp.dot(wcat, vcat, preferred_element_type=F32)

    prev = jnp.maximum(i - 1, 0)
    zb, fb = scores(i, past)
    za, fa = scores(prev, i > 0)
    tot_b = jnp.sum(fb, axis=1, keepdims=True)
    wb = weights(zb, fb, 0.0)
    wa = weights(za, fa, tot_b)
    carry0 = tot_b + jnp.sum(fa, axis=1, keepdims=True)
    acc0 = pv([wa, wb], [prev, i])

    def cond(st):
        j, m, _, _ = st
        return (j >= 0) & (m < SB_EXP_UNDERFLOW)

    def body(st):
        j, _, carry, acc = st
        z, fail = scores(j)
        w = weights(z, fail, carry)
        carry = carry + jnp.sum(fail, axis=1, keepdims=True)
        return j - 1, jnp.min(carry), carry, acc + pv([w], [j])

    _, _, _, acc = lax.while_loop(cond, body, (i - 2, jnp.min(carry0), carry0, acc0))
    o_ref[0] = acc.astype(o_ref.dtype)


def _mla_attn_kernel(bounded_ref, q_ref, k_ref, v_ref, o_ref, l_ref, acc_ref, *, tq):
    i = pl.program_id(2)
    q = q_ref[0]
    l2 = lax.broadcasted_iota(jnp.int32, (1, PAIR_W), 1)
    lr = l2 & 63
    hm0 = (l2 < 64) | ((l2 >= LANES) & (lr < 16))
    hm1 = ((l2 >= 64) & (l2 < LANES)) | ((l2 >= LANES) & (lr >= 16) & (lr < 32))
    zq = jnp.zeros_like(q)
    qcat = jnp.concatenate([jnp.where(hm0, q, zq), jnp.where(hm1, q, zq)], axis=0)
    key = lax.broadcasted_iota(jnp.int32, (tq, 2 * tq), 0)
    qry = lax.broadcasted_iota(jnp.int32, (tq, 2 * tq), 1) & (tq - 1)
    visible = (key // CHUNK) <= (qry // CHUNK)
    head0 = lax.broadcasted_iota(jnp.int32, (LANES, 1), 0) < 64

    def per_head(x):
        return jnp.where(head0, x[:, :tq], x[:, tq:])

    def logits(j, n, diag):
        start = pl.multiple_of(j * tq, tq)
        s = lax.dot_general(k_ref[0, pl.ds(start, n * tq), :], qcat, _TRANS_B, preferred_element_type=F32)
        if diag:
            vis = visible if n == 1 else jnp.concatenate([jnp.ones_like(visible)] * (n - 1) + [visible], axis=0)
            s = jnp.where(vis, s, -jnp.inf)
        return s

    def pv(j, n, p):
        vt = v_ref[:, pl.ds(pl.multiple_of(j * tq, tq), n * tq)]
        pb = p.astype(BF16)
        zv = jnp.zeros_like(vt)
        return jnp.dot(jnp.concatenate([jnp.where(head0, vt, zv), jnp.where(head0, zv, vt)], axis=1),
                       jnp.concatenate([pb[:, :tq], pb[:, tq:]], axis=0),
                       preferred_element_type=F32)

    def finish(l, acc):
        o_ref[0] = (acc / per_head(l)).T.astype(o_ref.dtype)

    @pl.when(bounded_ref[0] != 0)
    def _():
        def step(j, n, diag):
            p = jnp.exp2(logits(j, n, diag))
            l_ref[...] += jnp.sum(p, axis=0, keepdims=True)
            acc_ref[...] += pv(j, n, p)

        l_ref[...] = jnp.zeros_like(l_ref)
        acc_ref[...] = jnp.zeros_like(acc_ref)

        @pl.loop(0, i // 2)
        def _(t):
            step(2 * t, 2, False)

        @pl.when(i % 2 == 1)
        def _():
            step(i - 1, 2, True)

        @pl.when(i % 2 == 0)
        def _():
            step(i, 1, True)

        finish(l_ref[...], acc_ref[...])

    @pl.when(bounded_ref[0] == 0)
    def _():
        def step(j, st, diag):
            m, l, acc = st
            s = logits(j, 1, diag)
            m_new = jnp.maximum(m, jnp.max(s, axis=0, keepdims=True))
            alpha = jnp.exp2(m - m_new)
            p = jnp.exp2(s - m_new)
            l_new = alpha * l + jnp.sum(p, axis=0, keepdims=True)
            return m_new, l_new, per_head(alpha) * acc + pv(j, 1, p)

        st = step(i, (jnp.full((1, 2 * tq), -jnp.inf, F32), jnp.zeros((1, 2 * tq), F32),
                      jnp.zeros((LANES, tq), F32)), True)
        _, l, acc = lax.fori_loop(0, i, lambda j, st: step(j, st, False), st)
        finish(l, acc)


def _post_attn_kernel(x_ref, sb_ref, mla_ref, gsb_ref, gmla_ref, wo_ref, gmem_ref, wmq_ref, gqh_ref,
                      mk_ref, mv_ref, wmo_ref, gmoe_ref, wr_ref, br_ref,
                      h2g_ref, dest_ref, cnt_ref, carry_ref, *, n_tokens):
    d = x_ref.shape[-1]
    hd = d // MEM_HEADS
    tp = x_ref.shape[0]
    lane = lax.broadcasted_iota(jnp.int32, (1, LANES), 1).astype(F32)

    n_chunks = POST_ATTN_CHUNKS
    chunks = [pl.ds(c * (tp // n_chunks), tp // n_chunks) for c in range(n_chunks)]
    each = lambda f, *lists: [f(*args) for args in zip(*lists)]

    mixed = each(lambda r: jnp.concatenate(
        [_rms(sb_ref[r, :].astype(F32), gsb_ref[...]).astype(BF16),
         _rms(mla_ref[r, :].astype(F32), gmla_ref[...]).astype(BF16)], axis=1), chunks)
    h1 = each(lambda r, m: x_ref[r, :] + jnp.dot(m, wo_ref[...], preferred_element_type=F32), chunks, mixed)
    hq = each(lambda h: _rms(h, gmem_ref[...]).astype(BF16), h1)
    mq = each(lambda h: jnp.dot(h, wmq_ref[...], preferred_element_type=F32), hq)

    def mem_head(m, h):
        qh = (_rms(m[:, h * hd:(h + 1) * hd], gqh_ref[...]) * (hd ** -0.5)).astype(BF16)
        sc = lax.dot_general(qh, mk_ref[0, :, h * hd:(h + 1) * hd], _TRANS_B, preferred_element_type=F32)
        sc = sc - jnp.max(sc, axis=-1, keepdims=True)
        e = jnp.exp(sc)
        p = (e / jnp.sum(e, axis=-1, keepdims=True)).astype(BF16)
        return jnp.dot(p, mv_ref[0, :, h * hd:(h + 1) * hd], preferred_element_type=F32).astype(BF16)

    mos = [each(lambda m, h=h: mem_head(m, h), mq) for h in range(MEM_HEADS)]
    h2 = each(lambda h, *mo: h + jnp.dot(jnp.concatenate(mo, axis=1), wmo_ref[...],
                                         preferred_element_type=F32), h1, *mos)

    def route(rows, h2c):
        h2g_ref[rows, 0:d] = h2c
        t = _rms(h2c, gmoe_ref[...])
        t_hi = t.astype(BF16)
        t_lo = (t - t_hi.astype(F32)).astype(BF16)
        both = jnp.dot(t_hi, wr_ref[...], preferred_element_type=F32)
        logits = (both[:, :LANES] + both[:, LANES:]
                  + jnp.dot(t_lo, wr_ref[:, :LANES], preferred_element_type=F32)) + br_ref[...]

        big = float(LANES)
        neg = -jnp.inf
        lg = jnp.where(lane < N_GROUPS, logits, neg)
        gmax = jnp.max(lg, axis=-1, keepdims=True)
        g_idx = jnp.min(jnp.where(lg == gmax, lane, big), axis=-1, keepdims=True)
        g_w = 1.0 / jnp.sum(jnp.exp(lg - gmax), axis=-1, keepdims=True)
        e_lo = N_GROUPS + g_idx * EXPERTS_PER_GROUP
        in_grp = (lane >= e_lo) & (lane < e_lo + EXPERTS_PER_GROUP)
        le = jnp.where(in_grp, logits, neg)
        m1 = jnp.max(le, axis=-1, keepdims=True)
        i1 = jnp.min(jnp.where(le == m1, lane, big), axis=-1, keepdims=True)
        le2 = jnp.where(lane == i1, neg, le)
        m2 = jnp.max(le2, axis=-1, keepdims=True)
        i2 = jnp.min(jnp.where(le2 == m2, lane, big), axis=-1, keepdims=True)
        e2 = jnp.exp(m2 - m1)
        w1 = g_w / (1.0 + e2)
        w2 = g_w * e2 / (1.0 + e2)
        h2g_ref[rows, d:] = jnp.where(lane == i1, w1, 0.0) + jnp.where(lane == i2, w2, 0.0)

        ka = jnp.minimum(i1, i2) - e_lo
        kb = jnp.maximum(i1, i2) - e_lo
        pair = ka * (5.0 - ka) * 0.5 + kb - 1.0
        return g_idx * float(PAIRS_PER_GROUP) + pair

    bucket = jnp.concatenate(each(route, chunks, h2), axis=0)

    @pl.when(pl.program_id(0) == 0)
    def _():
        carry_ref[...] = jnp.zeros_like(carry_ref)

    oh = jnp.where(lane == bucket, 1.0, 0.0)
    earlier = (lax.broadcasted_iota(jnp.int32, (tp, tp), 1)
               < lax.broadcasted_iota(jnp.int32, (tp, tp), 0)).astype(BF16)
    prefix = jnp.dot(earlier, oh.astype(BF16), preferred_element_type=F32)
    carry = carry_ref[...]
    rank = jnp.sum(jnp.where(lane == bucket, prefix + carry, 0.0), axis=-1, keepdims=True)
    dest_ref[...] = (bucket * float(n_tokens) + rank).astype(jnp.int32)
    carry_ref[...] = carry + jnp.sum(oh, axis=0, keepdims=True)
    cnt_ref[0] = carry_ref[...]


def _scatter_rows_kernel(dest_ref, fill_start_ref, fill_n_ref, x_ref, o_hbm, zrow_ref, sem, zsem):
    i = pl.program_id(0)
    tg = x_ref.shape[0]
    base = i * tg

    @pl.loop(0, tg // SUBLANES)
    def _(r8):
        r0 = pl.multiple_of(r8 * SUBLANES, SUBLANES)
        for k in range(SUBLANES):
            pltpu.make_async_copy(x_ref.at[pl.ds(r0 + k, 1)],
                                  o_hbm.at[pl.ds(dest_ref[base + r0 + k], 1)], sem).start()

    pltpu.make_async_copy(x_ref, o_hbm.at[pl.ds(0, tg)], sem).wait()

    @pl.when(i == pl.num_programs(0) - 1)
    def _():
        zrow_ref[...] = jnp.zeros_like(zrow_ref)

        def fill(b, act):
            n = fill_n_ref[b]
            pos = fill_start_ref[b]
            head = jnp.minimum((-pos) & (SUBLANES - 1), n)
            zcopy = lambda at, size: pltpu.make_async_copy(zrow_ref.at[pl.ds(0, size)],
                                                           o_hbm.at[pl.ds(at, size)], zsem)
            for k in range(SUBLANES - 1):
                pl.when(k < head)(functools.partial(act, zcopy(pos + k, 1)))
            pos = pos + head
            n = n - head
            size = zrow_ref.shape[0]
            while size >= SUBLANES:
                pl.when((n & size) != 0)(functools.partial(act, zcopy(pl.multiple_of(pos, SUBLANES), size)))
                pos = pos + (n & size)
                size //= 2

        pl.loop(0, fill_n_ref.shape[0])(lambda b: fill(b, lambda c: c.start()))
        pl.loop(0, fill_n_ref.shape[0])(lambda b: fill(b, lambda c: c.wait()))


def _moe_pair_kernel(blk_ref, ea_ref, eb_ref, nvalid_ref, x_ref, gmoe_ref,
                     wga_ref, wua_ref, wda_ref, wgb_ref, wub_ref, wdb_ref, o_ref):
    i = pl.program_id(0)
    d = o_ref.shape[-1]

    @pl.when(i < nvalid_ref[0])
    def _():
        x = x_ref[...]
        h2 = x[:, 0:d]
        gates = x[:, d:]
        t = _rms(h2, gmoe_ref[...]).astype(BF16)
        lane = lax.broadcasted_iota(jnp.int32, (1, LANES), 1)

        def hidden(e, wg_ref, wu_ref):
            ge = jnp.sum(jnp.where(lane == N_GROUPS + e, gates, 0.0), axis=-1, keepdims=True)
            a = jnp.dot(t, wg_ref[0], preferred_element_type=F32)
            u = jnp.dot(t, wu_ref[0], preferred_element_type=F32)
            return (a * jax.nn.sigmoid(a) * u * ge).astype(BF16)

        o_ref[...] = (h2 + jnp.dot(hidden(ea_ref[i], wga_ref, wua_ref), wda_ref[0], preferred_element_type=F32)
                      + jnp.dot(hidden(eb_ref[i], wgb_ref, wub_ref), wdb_ref[0], preferred_element_type=F32))


def _gather_rows_kernel(tile_ref, sub_ref, y_hbm, o_ref, sem):
    n8 = o_ref.shape[0]
    base = pl.program_id(0) * n8 * SUBLANES

    @pl.loop(0, n8)
    def _(r8):
        r0 = base + r8 * SUBLANES
        for k in range(SUBLANES):
            pltpu.make_async_copy(y_hbm.at[tile_ref[r0 + k], pl.ds(sub_ref[r0 + k] & (SUBLANES - 1), 1)],
                                  o_ref.at[r8, pl.ds(k, 1)], sem).start()

    pltpu.make_async_copy(y_hbm.at[pl.ds(0, n8)], o_ref, sem).wait()


def _rope_block(x1, x2):
    z = jnp.zeros(x1.shape[:-1] + (32,), x1.dtype)
    return jnp.concatenate([x1, x1, z, x2, x2, z], axis=-1)


def _pair_rope_block(a, b):
    z = jnp.zeros(a.shape[:-1] + (32,), a.dtype)
    return jnp.concatenate([a[..., :16], b[..., :16], z, a[..., 16:], b[..., 16:], z], axis=-1)


def _row(v):
    return v.reshape(1, -1).astype(F32)


def kernel(x, mem, positions, norm_mix_g, w_in, mla_q_norm_g, w_uq, mla_kv_norm_g, w_ukv, mla_qn_g, mla_qr_g, mla_kn_g, mla_kr_g, sb_out_g, mla_out_g, w_o, norm_mem_g, mem_src_g, w_mq, w_mkv, mem_qhead_g, mem_khead_g, w_mo, norm_moe_g, w_group, b_group, w_router, b_router, w_gate, w_up, w_down):
    B, S, D = x.shape
    T = B * S
    M = mem.shape[1]
    FF = w_gate.shape[-1]
    half = MLA_ROPE // 2
    qk = MLA_NOPE + MLA_ROPE

    o = 3 * SB_WIDTH + MLA_Q_RANK + MLA_KV_RANK
    w_sb = jnp.concatenate([w_in[:, :SB_WIDTH] * (SB_HEAD_DIM ** -0.5),
                            w_in[:, SB_WIDTH:3 * SB_WIDTH]], axis=1).astype(BF16)
    w_lat = jnp.concatenate([w_in[:, 3 * SB_WIDTH:o],
                             _rope_block(w_in[:, o:o + half], w_in[:, o + half:])], axis=1).astype(BF16)
    uq = w_uq.reshape(MLA_Q_RANK, MLA_HEADS, qk)
    ukv = w_ukv.reshape(MLA_KV_RANK, MLA_HEADS, MLA_NOPE + MLA_V)
    zk = jnp.zeros((MLA_KV_RANK, LANES), F32)
    uq_cols, uk_cols = [], []
    for p in range(N_PAIRS):
        a, b = 2 * p, 2 * p + 1
        uq_cols += [uq[:, a, :MLA_NOPE], uq[:, b, :MLA_NOPE], _pair_rope_block(uq[:, a, MLA_NOPE:], uq[:, b, MLA_NOPE:])]
        uk_cols += [ukv[:, a, :MLA_NOPE], ukv[:, b, :MLA_NOPE], zk]
    wuq = jnp.concatenate(uq_cols, axis=1).astype(BF16)
    wukv = jnp.concatenate(uk_cols + [ukv[:, h, MLA_NOPE:] for h in range(MLA_HEADS)], axis=1).astype(BF16)
    gq_pair = jnp.concatenate([mla_qn_g, mla_qn_g, _pair_rope_block(mla_qr_g, mla_qr_g)])
    gk_pair = jnp.concatenate([mla_kn_g, mla_kn_g, jnp.zeros((LANES,), F32)])
    gqcat = _row(jnp.tile(gq_pair, N_PAIRS))
    gkcat = _row(jnp.tile(gk_pair, N_PAIRS))
    gkr = _row(_rope_block(mla_kr_g[:half], mla_kr_g[half:]))
    inv_freq = ROPE_THETA ** (-(jnp.arange(half, dtype=F32) * 2.0 / MLA_ROPE))
    invf = _row(_rope_block(inv_freq, inv_freq))
    sgn = _row(jnp.concatenate([-jnp.ones((64,), F32), jnp.ones((64,), F32)]))
    wr = jnp.concatenate([w_group, w_router, jnp.zeros((D, LANES - N_GROUPS - N_EXPERTS), F32)], axis=1)
    wr_hi = wr.astype(BF16)
    wr_cat = jnp.concatenate([wr_hi, (wr - wr_hi.astype(F32)).astype(BF16)], axis=1)
    br = _row(jnp.concatenate([b_group, b_router, jnp.zeros((LANES - N_GROUPS - N_EXPERTS,), F32)]))

    full = lambda shape: pl.BlockSpec(shape, lambda *_: (0,) * len(shape))

    mk, mv = pl.pallas_call(
        _mem_kv_kernel, grid=(B,), name="mem_kv",
        in_specs=[pl.BlockSpec((1, M, D), lambda b: (b, 0, 0)), full((1, D)), full((D, 2 * D)),
                  full((1, D // MEM_HEADS))],
        out_specs=[pl.BlockSpec((1, M, D), lambda b: (b, 0, 0))] * 2,
        out_shape=[jax.ShapeDtypeStruct((B, M, D), BF16)] * 2,
    )(mem, _row(mem_src_g), w_mkv.astype(BF16), _row(mem_khead_g))

    tm = 256
    rows = lambda w: pl.BlockSpec((tm, w), lambda i: (i, 0))
    sbq, sbk, sbv, mq, mkk, mvv = pl.pallas_call(
        _in_proj_kernel, grid=(T // tm,), name="in_proj",
        in_specs=[rows(D), rows(1), full((1, D)), full(w_lat.shape), full(w_sb.shape),
                  full((1, MLA_Q_RANK)), full(wuq.shape),
                  full((1, MLA_KV_RANK)), full(wukv.shape), full(gqcat.shape), full(gkcat.shape),
                  full((1, LANES)), full((1, LANES)), full((1, LANES))],
        out_specs=[rows(SB_WIDTH)] * 3 + [rows(N_PAIRS * PAIR_W)] * 2
                  + [pl.BlockSpec((MLA_WIDTH, tm), lambda i: (0, i))],
        out_shape=[jax.ShapeDtypeStruct((T, SB_WIDTH), BF16)] * 3
                  + [jax.ShapeDtypeStruct((T, N_PAIRS * PAIR_W), BF16)] * 2
                  + [jax.ShapeDtypeStruct((MLA_WIDTH, T), BF16)],
        scratch_shapes=[pltpu.VMEM((tm, LANES), F32)] * 2,
    )(x.reshape(T, D), positions.reshape(T, 1), _row(norm_mix_g), w_lat, w_sb, _row(mla_q_norm_g), wuq,
      _row(mla_kv_norm_g), wukv, gqcat, gkcat, gkr, invf, sgn)

    qspec = lambda t, w: pl.BlockSpec((1, t, w), lambda b, p, i: (b, i, p))
    kvspec = lambda w: pl.BlockSpec((1, S, w), lambda b, p, i: (b, 0, p))
    tq = 256
    sb = pl.pallas_call(
        functools.partial(_sb_attn_kernel, tq=tq), grid=(B, SB_HEADS // 2, S // tq), name="sb_attn",
        in_specs=[qspec(tq, LANES), kvspec(LANES), kvspec(LANES)],
        out_specs=qspec(tq, LANES),
        out_shape=jax.ShapeDtypeStruct((B, S, SB_WIDTH), BF16),
    )(sbq.reshape(B, S, SB_WIDTH), sbk.reshape(B, S, SB_WIDTH), sbv.reshape(B, S, SB_WIDTH))
    tqm = 512
    gmax2 = lambda g: jnp.max(jnp.square(g))
    q_bound = jnp.sqrt(MLA_NOPE * gmax2(mla_qn_g) + MLA_ROPE * gmax2(mla_qr_g))
    k_bound = jnp.sqrt(MLA_NOPE * gmax2(mla_kn_g) + MLA_ROPE * gmax2(mla_kr_g))
    logit_bound = 1.02 * q_bound * k_bound * (qk ** -0.5) * LOG2E
    bounded = (logit_bound < MLA_SAFE_LOGIT).astype(jnp.int32).reshape(1)
    mla = pl.pallas_call(
        functools.partial(_mla_attn_kernel, tq=tqm), name="mla_attn",
        grid_spec=pltpu.PrefetchScalarGridSpec(
            num_scalar_prefetch=1, grid=(B, N_PAIRS, S // tqm),
            in_specs=[pl.BlockSpec((1, tqm, PAIR_W), lambda b, p, i, *_: (b, i, p)),
                      pl.BlockSpec((1, S, PAIR_W), lambda b, p, i, *_: (b, 0, p)),
                      pl.BlockSpec((LANES, S), lambda b, p, i, *_: (p, b))],
            out_specs=pl.BlockSpec((1, tqm, LANES), lambda b, p, i, *_: (b, i, p)),
            scratch_shapes=[pltpu.VMEM((1, 2 * tqm), F32), pltpu.VMEM((LANES, tqm), F32)]),
        out_shape=jax.ShapeDtypeStruct((B, S, MLA_WIDTH), BF16),
    )(bounded, mq.reshape(B, S, -1), mkk.reshape(B, S, -1), mvv)

    tp = 1024
    prow = lambda w: pl.BlockSpec((tp, w), lambda i: (i, 0))
    memspec = pl.BlockSpec((1, M, D), lambda i: ((i * tp) // S, 0, 0))
    W = D + LANES
    h2g, dest, cnt = pl.pallas_call(
        functools.partial(_post_attn_kernel, n_tokens=T), grid=(T // tp,), name="post_attn",
        in_specs=[prow(D), prow(SB_WIDTH), prow(MLA_WIDTH), full((1, SB_WIDTH)), full((1, MLA_WIDTH)),
                  full((D, D)), full((1, D)), full((D, D)), full((1, D // MEM_HEADS)), memspec, memspec,
                  full((D, D)), full((1, D)), full((D, 2 * LANES)), full((1, LANES))],
        out_specs=[prow(W), prow(1), pl.BlockSpec((1, 1, LANES), lambda i: (i, 0, 0))],
        out_shape=[jax.ShapeDtypeStruct((T, W), F32), jax.ShapeDtypeStruct((T, 1), jnp.int32),
                   jax.ShapeDtypeStruct((T // tp, 1, LANES), F32)],
        scratch_shapes=[pltpu.VMEM((1, LANES), F32)],
        compiler_params=pltpu.CompilerParams(dimension_semantics=("arbitrary",)),
    )(x.reshape(T, D), sb.reshape(T, SB_WIDTH), mla.reshape(T, MLA_WIDTH), _row(sb_out_g), _row(mla_out_g),
      w_o.astype(BF16), _row(norm_mem_g), w_mq.astype(BF16), _row(mem_qhead_g), mk, mv,
      w_mo.astype(BF16), _row(norm_moe_g), wr_cat, br)

    te = 256
    tg = 2048
    n_buckets = N_GROUPS * PAIRS_PER_GROUP
    n_steps = T // te + n_buckets
    counts = cnt[-1, 0, :n_buckets].astype(jnp.int32)
    tiles = (counts + te - 1) // te
    ends = jnp.cumsum(tiles)
    first_row = (ends - tiles) * te
    n_valid = ends[-1]
    slot = dest.reshape(T // LANES, LANES)
    slot_bucket = slot // T
    row0 = jnp.zeros_like(slot)
    for b in range(n_buckets):
        row0 = jnp.where(slot_bucket == b, first_row[b], row0)
    dest = (row0 + slot % T).reshape(T)
    step = jnp.minimum(jnp.arange(n_steps, dtype=jnp.int32), n_valid - 1)
    bucket = jnp.zeros_like(step)
    for b in range(n_buckets):
        bucket = bucket + (step >= ends[b]).astype(jnp.int32)
    pr = bucket % PAIRS_PER_GROUP
    pair_a = (pr >= 3).astype(jnp.int32) + (pr >= 5).astype(jnp.int32)
    pair_b = pr + 1 - 2 * (pr >= 3).astype(jnp.int32) - (pr >= 5).astype(jnp.int32)
    ea = bucket // PAIRS_PER_GROUP * EXPERTS_PER_GROUP + pair_a
    eb = bucket // PAIRS_PER_GROUP * EXPERTS_PER_GROUP + pair_b

    any_spec = pl.BlockSpec(memory_space=pl.ANY)
    h2s = pl.pallas_call(
        _scatter_rows_kernel, name="moe_dispatch",
        grid_spec=pltpu.PrefetchScalarGridSpec(
            num_scalar_prefetch=3, grid=(T // tg,),
            in_specs=[pl.BlockSpec((tg, W), lambda i, *_: (i, 0))],
            out_specs=any_spec,
            scratch_shapes=[pltpu.VMEM((te // 2, W), F32),
                            pltpu.SemaphoreType.DMA, pltpu.SemaphoreType.DMA]),
        out_shape=jax.ShapeDtypeStruct((n_steps * te, W), F32),
        compiler_params=pltpu.CompilerParams(dimension_semantics=("arbitrary",)),
    )(dest, first_row + counts, tiles * te - counts, h2g)

    wspec = lambda shape, which: pl.BlockSpec((1,) + shape, lambda i, blk, ea, eb, nv: ((ea, eb)[which][i], 0, 0))
    w_gate_b, w_up_b, w_down_b = w_gate.astype(BF16), w_up.astype(BF16), w_down.astype(BF16)
    ys = pl.pallas_call(
        _moe_pair_kernel, name="moe_experts",
        grid_spec=pltpu.PrefetchScalarGridSpec(
            num_scalar_prefetch=4, grid=(n_steps,),
            in_specs=[pl.BlockSpec((te, W), lambda i, blk, *_: (blk[i], 0)),
                      pl.BlockSpec((1, D), lambda i, *_: (0, 0)),
                      wspec((D, FF), 0), wspec((D, FF), 0), wspec((FF, D), 0),
                      wspec((D, FF), 1), wspec((D, FF), 1), wspec((FF, D), 1)],
            out_specs=pl.BlockSpec((te, D), lambda i, blk, *_: (blk[i], 0))),
        out_shape=jax.ShapeDtypeStruct((n_steps * te, D), F32),
        compiler_params=pltpu.CompilerParams(dimension_semantics=("arbitrary",)),
    )(step, ea, eb, n_valid.reshape(1), h2s, _row(norm_moe_g),
      w_gate_b, w_up_b, w_down_b, w_gate_b, w_up_b, w_down_b)

    out = pl.pallas_call(
        _gather_rows_kernel, name="moe_combine",
        grid_spec=pltpu.PrefetchScalarGridSpec(
            num_scalar_prefetch=2, grid=(T // tg,),
            in_specs=[any_spec],
            out_specs=pl.BlockSpec((tg // SUBLANES, SUBLANES, D), lambda i, *_: (i, 0, 0)),
            scratch_shapes=[pltpu.SemaphoreType.DMA]),
        out_shape=jax.ShapeDtypeStruct((T // SUBLANES, SUBLANES, D), F32),
        compiler_params=pltpu.CompilerParams(dimension_semantics=("arbitrary",)),
    )(dest // SUBLANES, dest % SUBLANES, ys.reshape(-1, SUBLANES, D))
    return out.reshape(B, S, D)
```

```python
import functools

import numpy as np
import jax
import jax.numpy as jnp
from jax import lax
from jax.experimental import pallas as pl
from jax.experimental.pallas import tpu as pltpu

F32 = jnp.float32
BF16 = jnp.bfloat16
EPS = 1e-6
LANES = 128
SUBLANES = 8

CHUNK = 64
SB_HEADS = 8
SB_HEAD_DIM = 64
SB_WIDTH = SB_HEADS * SB_HEAD_DIM
MLA_HEADS = 8
MLA_NOPE = 64
MLA_ROPE = 32
MLA_V = 64
MLA_WIDTH = MLA_HEADS * MLA_V
MLA_Q_RANK = 384
MLA_KV_RANK = 256
ROPE_THETA = 10000.0
MEM_HEADS = 4
N_GROUPS = 4
EXPERTS_PER_GROUP = 4
N_EXPERTS = N_GROUPS * EXPERTS_PER_GROUP
PAIRS_PER_GROUP = EXPERTS_PER_GROUP * (EXPERTS_PER_GROUP - 1) // 2
POST_ATTN_CHUNKS = 2
N_PAIRS = MLA_HEADS // 2
PAIR_W = 2 * LANES
SB_EXP_UNDERFLOW = 104.0
SB_DROPPED = -1e30
LOG2E = 1.4426950408889634
MLA_SAFE_LOGIT = 60.0

_TRANS_B = (((1,), (1,)), ((), ()))


def _rms(x, g):
    return x * lax.rsqrt(jnp.mean(x * x, axis=-1, keepdims=True) + EPS) * g


def _seg_rms(x, masks, seg_len, g):
    x2 = x * x
    r = jnp.zeros_like(x)
    for m in masks:
        s = jnp.sum(jnp.where(m, x2, 0.0), axis=-1, keepdims=True)
        r = jnp.where(m, lax.rsqrt(s * (1.0 / seg_len) + EPS), r)
    return x * r * g


def _mem_kv_kernel(mem_ref, gsrc_ref, w_ref, gk_ref, mk_ref, mv_ref):
    d = mem_ref.shape[-1]
    hd = d // MEM_HEADS
    mn = _rms(mem_ref[0], gsrc_ref[...]).astype(BF16)
    kv = jnp.dot(mn, w_ref[...], preferred_element_type=F32)
    for h in range(MEM_HEADS):
        kh = kv[:, h * hd:(h + 1) * hd]
        mk_ref[0, :, h * hd:(h + 1) * hd] = _rms(kh, gk_ref[...]).astype(BF16)
    mv_ref[0] = kv[:, d:].astype(BF16)


def _in_proj_kernel(x_ref, pos_ref, gmix_ref, wlat_ref, wsb_ref, gq_ref, wuq_ref, gkv_ref, wukv_ref,
                    gqcat_ref, gkcat_ref, gkr_ref, invf_ref, sgn_ref,
                    sbq_ref, sbk_ref, sbv_ref, mq_ref, mk_ref, mv_ref, cos_ref, sin_ref):
    ang = pos_ref[...].astype(F32) * invf_ref[...]
    cos_ref[...] = jnp.cos(ang)
    sin_ref[...] = jnp.sin(ang) * sgn_ref[...]

    hn = _rms(x_ref[...], gmix_ref[...]).astype(BF16)
    lat = jnp.dot(hn, wlat_ref[...], preferred_element_type=F32)
    cq = lat[:, 0:MLA_Q_RANK]
    ckv = lat[:, MLA_Q_RANK:MLA_Q_RANK + MLA_KV_RANK]
    kr = lat[:, MLA_Q_RANK + MLA_KV_RANK:]
    q = jnp.dot(_rms(cq, gq_ref[...]).astype(BF16), wuq_ref[...], preferred_element_type=F32)
    kv = jnp.dot(_rms(ckv, gkv_ref[...]).astype(BF16), wukv_ref[...], preferred_element_type=F32)

    sb = jnp.dot(hn, wsb_ref[...], preferred_element_type=F32)
    sbq_ref[...] = sb[:, 0:SB_WIDTH].astype(BF16)
    sbk_ref[...] = sb[:, SB_WIDTH:2 * SB_WIDTH].astype(BF16)
    sbv_ref[...] = sb[:, 2 * SB_WIDTH:3 * SB_WIDTH].astype(BF16)

    lane = lax.broadcasted_iota(jnp.int32, (1, LANES), 1)
    lr = lane & 63
    lo, hi = lane < 64, lane >= 64
    r0, r1 = lr < 16, (lr >= 16) & (lr < 32)

    def rope(b):
        return b * cos_ref[...] + pltpu.roll(b, 64, 1) * sin_ref[...]

    q_scale = (MLA_NOPE + MLA_ROPE) ** -0.5 * LOG2E
    krn = rope(_seg_rms(kr, (r0, r1), MLA_ROPE, gkr_ref[...]))
    for p in range(N_PAIRS):
        b0 = p * PAIR_W
        qa = _seg_rms(q[:, b0:b0 + LANES], (lo, hi), MLA_NOPE, gqcat_ref[:, b0:b0 + LANES])
        qb = rope(_seg_rms(q[:, b0 + LANES:b0 + PAIR_W], (r0, r1), MLA_ROPE,
                           gqcat_ref[:, b0 + LANES:b0 + PAIR_W]))
        mq_ref[:, b0:b0 + LANES] = (qa * q_scale).astype(BF16)
        mq_ref[:, b0 + LANES:b0 + PAIR_W] = (qb * q_scale).astype(BF16)
        ka = _seg_rms(kv[:, b0:b0 + LANES], (lo, hi), MLA_NOPE, gkcat_ref[:, b0:b0 + LANES])
        mk_ref[:, b0:b0 + LANES] = ka.astype(BF16)
        mk_ref[:, b0 + LANES:b0 + PAIR_W] = krn.astype(BF16)
    mv_ref[...] = kv[:, N_PAIRS * PAIR_W:].T.astype(BF16)


def _sb_attn_kernel(q_ref, k_ref, v_ref, o_ref, *, tq):
    i = pl.program_id(2)
    q = q_ref[0]
    lane = lax.broadcasted_iota(jnp.int32, (1, LANES), 1)
    lo = lane < 64
    zq = jnp.zeros_like(q)
    qcat = jnp.concatenate([jnp.where(lo, q, zq), jnp.where(lo, zq, q)], axis=0)
    row = lax.broadcasted_iota(jnp.int32, (tq, tq), 0)
    col = lax.broadcasted_iota(jnp.int32, (tq, tq), 1)
    later = (row > col).astype(BF16)
    past = jnp.concatenate([col < row, col < row], axis=0)

    def scores(j, mask=None):
        start = pl.multiple_of(j * tq, tq)
        z = lax.dot_general(qcat, k_ref[0, pl.ds(start, tq), :], _TRANS_B, preferred_element_type=F32)
        if mask is not None:
            z = mask(z)
        return z, jnp.maximum(z, 0.0) + jnp.log(1.0 + jnp.exp(-jnp.abs(z)))

    def weights(z, fail, after):
        between = after + jnp.dot(fail.astype(BF16), later, preferred_element_type=F32)
        return jnp.exp((z - fail) - between).astype(BF16)

    def pv(ws, js):
        zv = jnp.zeros((tq, LANES), BF16)
        vs = [v_ref[0, pl.ds(pl.multiple_of(j * tq, tq), tq), :] for j in js]
        wcat = jnp.concatenate([w[:tq] for w in ws] + [w[tq:] for w in ws], axis=1)
        vcat = jnp.concatenate([jnp.where(lo, v, zv) for v in vs] + [jnp.where(lo, zv, v) for v in vs], axis=0)
        return jnp.dot(wcat, vcat, preferred_element_type=F32)

    prev = jnp.maximum(i - 1, 0)
    zb, fb = scores(i, lambda z: jnp.where(past, z, SB_DROPPED))
    za, fa = scores(prev, lambda z: jnp.where(i > 0, z, SB_DROPPED))
    tot_b = jnp.sum(fb, axis=1, keepdims=True)
    wb = weights(zb, fb, 0.0)
    wa = weights(za, fa, tot_b)
    carry0 = tot_b + jnp.sum(fa, axis=1, keepdims=True)
    acc0 = pv([wa, wb], [prev, i])

    def cond(st):
        j, m, _, _ = st
        return (j >= 0) & (m < SB_EXP_UNDERFLOW)

    def body(st):
        j, _, carry, acc = st
        z, fail = scores(j)
        w = weights(z, fail, carry)
        carry = carry + jnp.sum(fail, axis=1, keepdims=True)
        return j - 1, jnp.min(carry), carry, acc + pv([w], [j])

    _, _, _, acc = lax.while_loop(cond, body, (i - 2, jnp.min(carry0), carry0, acc0))
    o_ref[0] = acc.astype(o_ref.dtype)


def _mla_attn_kernel(bounded_ref, q_ref, k_ref, v_ref, diag_ref, o_ref, l_ref, acc_ref, *, tq):
    i = pl.program_id(2)
    q = q_ref[0]
    l2 = lax.broadcasted_iota(jnp.int32, (1, PAIR_W), 1)
    lr = l2 & 63
    hm0 = (l2 < 64) | ((l2 >= LANES) & (lr < 16))
    hm1 = ((l2 >= 64) & (l2 < LANES)) | ((l2 >= LANES) & (lr >= 16) & (lr < 32))
    zq = jnp.zeros_like(q)
    qcat = jnp.concatenate([jnp.where(hm0, q, zq), jnp.where(hm1, q, zq)], axis=0)
    head0 = lax.broadcasted_iota(jnp.int32, (LANES, 1), 0) < 64

    def per_head(x):
        return jnp.where(head0, x[:, :tq], x[:, tq:])

    def logits(j, n, diag):
        start = pl.multiple_of(j * tq, tq)
        s = lax.dot_general(k_ref[0, pl.ds(start, n * tq), :], qcat, _TRANS_B, preferred_element_type=F32)
        if diag:
            tiles = [s[t * tq:(t + 1) * tq] for t in range(n)]
            s = jnp.concatenate(tiles[:-1] + [tiles[-1] + diag_ref[...]], axis=0)
        return s

    def pv(j, n, p):
        vt = v_ref[:, pl.ds(pl.multiple_of(j * tq, tq), n * tq)]
        pb = p.astype(BF16)
        zv = jnp.zeros_like(vt)
        return jnp.dot(jnp.concatenate([jnp.where(head0, vt, zv), jnp.where(head0, zv, vt)], axis=1),
                       jnp.concatenate([pb[:, :tq], pb[:, tq:]], axis=0),
                       preferred_element_type=F32)

    def finish(l, acc):
        o_ref[0] = (acc / per_head(l)).T.astype(o_ref.dtype)

    @pl.when(bounded_ref[0] != 0)
    def _():
        def step(j, n, diag):
            p = jnp.exp2(logits(j, n, diag))
            l_ref[...] += jnp.sum(p, axis=0, keepdims=True)
            acc_ref[...] += pv(j, n, p)

        l_ref[...] = jnp.zeros_like(l_ref)
        acc_ref[...] = jnp.zeros_like(acc_ref)

        @pl.loop(0, i // 4)
        def _(t):
            step(4 * t, 4, False)

        @pl.when(i % 4 >= 2)
        def _():
            step(i // 4 * 4, 2, False)

        @pl.when(i % 2 == 1)
        def _():
            step(i - 1, 2, True)

        @pl.when(i % 2 == 0)
        def _():
            step(i, 1, True)

        finish(l_ref[...], acc_ref[...])

    @pl.when(bounded_ref[0] == 0)
    def _():
        def step(j, st, diag):
            m, l, acc = st
            s = logits(j, 1, diag)
            m_new = jnp.maximum(m, jnp.max(s, axis=0, keepdims=True))
            alpha = jnp.exp2(m - m_new)
            p = jnp.exp2(s - m_new)
            l_new = alpha * l + jnp.sum(p, axis=0, keepdims=True)
            return m_new, l_new, per_head(alpha) * acc + pv(j, 1, p)

        st = step(i, (jnp.full((1, 2 * tq), -jnp.inf, F32), jnp.zeros((1, 2 * tq), F32),
                      jnp.zeros((LANES, tq), F32)), True)
        _, l, acc = lax.fori_loop(0, i, lambda j, st: step(j, st, False), st)
        finish(l, acc)


def _post_attn_kernel(x_ref, sb_ref, mla_ref, gsb_ref, gmla_ref, wo_ref, gmem_ref, wmq_ref, gqh_ref,
                      mk_ref, mv_ref, wmo_ref, gmoe_ref, wr_ref, br_ref,
                      h2g_ref, dest_ref, cnt_ref, carry_ref, *, n_tokens):
    d = x_ref.shape[-1]
    hd = d // MEM_HEADS
    tp = x_ref.shape[0]
    lane = lax.broadcasted_iota(jnp.int32, (1, LANES), 1).astype(F32)

    n_chunks = POST_ATTN_CHUNKS
    chunks = [pl.ds(c * (tp // n_chunks), tp // n_chunks) for c in range(n_chunks)]
    each = lambda f, *lists: [f(*args) for args in zip(*lists)]

    mixed = each(lambda r: jnp.concatenate(
        [_rms(sb_ref[r, :].astype(F32), gsb_ref[...]).astype(BF16),
         _rms(mla_ref[r, :].astype(F32), gmla_ref[...]).astype(BF16)], axis=1), chunks)
    h1 = each(lambda r, m: x_ref[r, :] + jnp.dot(m, wo_ref[...], preferred_element_type=F32), chunks, mixed)
    hq = each(lambda h: _rms(h, gmem_ref[...]).astype(BF16), h1)
    mq = each(lambda h: jnp.dot(h, wmq_ref[...], preferred_element_type=F32), hq)

    def mem_head(m, h):
        qh = (_rms(m[:, h * hd:(h + 1) * hd], gqh_ref[...]) * (hd ** -0.5)).astype(BF16)
        sc = lax.dot_general(qh, mk_ref[0, :, h * hd:(h + 1) * hd], _TRANS_B, preferred_element_type=F32)
        sc = sc - jnp.max(sc, axis=-1, keepdims=True)
        e = jnp.exp(sc)
        p = (e / jnp.sum(e, axis=-1, keepdims=True)).astype(BF16)
        return jnp.dot(p, mv_ref[0, :, h * hd:(h + 1) * hd], preferred_element_type=F32).astype(BF16)

    mos = [each(lambda m, h=h: mem_head(m, h), mq) for h in range(MEM_HEADS)]
    h2 = each(lambda h, *mo: h + jnp.dot(jnp.concatenate(mo, axis=1), wmo_ref[...],
                                         preferred_element_type=F32), h1, *mos)

    def route(rows, h2c):
        h2g_ref[rows, 0:d] = h2c
        t = _rms(h2c, gmoe_ref[...])
        t_hi = t.astype(BF16)
        t_lo = (t - t_hi.astype(F32)).astype(BF16)
        both = jnp.dot(t_hi, wr_ref[...], preferred_element_type=F32)
        logits = (both[:, :LANES] + both[:, LANES:]
                  + jnp.dot(t_lo, wr_ref[:, :LANES], preferred_element_type=F32)) + br_ref[...]

        big = float(LANES)
        neg = -jnp.inf
        lg = jnp.where(lane < N_GROUPS, logits, neg)
        gmax = jnp.max(lg, axis=-1, keepdims=True)
        g_idx = jnp.min(jnp.where(lg == gmax, lane, big), axis=-1, keepdims=True)
        g_w = 1.0 / jnp.sum(jnp.exp(lg - gmax), axis=-1, keepdims=True)
        e_lo = N_GROUPS + g_idx * EXPERTS_PER_GROUP
        in_grp = (lane >= e_lo) & (lane < e_lo + EXPERTS_PER_GROUP)
        le = jnp.where(in_grp, logits, neg)
        m1 = jnp.max(le, axis=-1, keepdims=True)
        i1 = jnp.min(jnp.where(le == m1, lane, big), axis=-1, keepdims=True)
        le2 = jnp.where(lane == i1, neg, le)
        m2 = jnp.max(le2, axis=-1, keepdims=True)
        i2 = jnp.min(jnp.where(le2 == m2, lane, big), axis=-1, keepdims=True)
        e2 = jnp.exp(m2 - m1)
        w1 = g_w / (1.0 + e2)
        w2 = g_w * e2 / (1.0 + e2)
        h2g_ref[rows, d:] = jnp.where(lane == i1, w1, 0.0) + jnp.where(lane == i2, w2, 0.0)

        ka = jnp.minimum(i1, i2) - e_lo
        kb = jnp.maximum(i1, i2) - e_lo
        pair = ka * (5.0 - ka) * 0.5 + kb - 1.0
        return g_idx * float(PAIRS_PER_GROUP) + pair

    bucket = jnp.concatenate(each(route, chunks, h2), axis=0)

    @pl.when(pl.program_id(0) == 0)
    def _():
        carry_ref[...] = jnp.zeros_like(carry_ref)

    oh = jnp.where(lane == bucket, 1.0, 0.0)
    earlier = (lax.broadcasted_iota(jnp.int32, (tp, tp), 1)
               < lax.broadcasted_iota(jnp.int32, (tp, tp), 0)).astype(BF16)
    prefix = jnp.dot(earlier, oh.astype(BF16), preferred_element_type=F32)
    carry = carry_ref[...]
    rank = jnp.sum(jnp.where(lane == bucket, prefix + carry, 0.0), axis=-1, keepdims=True)
    dest_ref[...] = (bucket * float(n_tokens) + rank).astype(jnp.int32)
    carry_ref[...] = carry + jnp.sum(oh, axis=0, keepdims=True)
    cnt_ref[0] = carry_ref[...]


def _scatter_rows_kernel(dest_ref, fill_start_ref, fill_n_ref, x_ref, o_hbm, zrow_ref, sem, zsem):
    i = pl.program_id(0)
    tg = x_ref.shape[0]
    base = i * tg

    @pl.loop(0, tg // SUBLANES)
    def _(r8):
        r0 = pl.multiple_of(r8 * SUBLANES, SUBLANES)
        for k in range(SUBLANES):
            pltpu.make_async_copy(x_ref.at[pl.ds(r0 + k, 1)],
                                  o_hbm.at[pl.ds(dest_ref[base + r0 + k], 1)], sem).start()

    pltpu.make_async_copy(x_ref, o_hbm.at[pl.ds(0, tg)], sem).wait()

    @pl.when(i == pl.num_programs(0) - 1)
    def _():
        zrow_ref[...] = jnp.zeros_like(zrow_ref)

        def fill(b, act):
            n = fill_n_ref[b]
            pos = fill_start_ref[b]
            head = jnp.minimum((-pos) & (SUBLANES - 1), n)
            zcopy = lambda at, size: pltpu.make_async_copy(zrow_ref.at[pl.ds(0, size)],
                                                           o_hbm.at[pl.ds(at, size)], zsem)
            for k in range(SUBLANES - 1):
                pl.when(k < head)(functools.partial(act, zcopy(pos + k, 1)))
            pos = pos + head
            n = n - head
            size = zrow_ref.shape[0]
            while size >= SUBLANES:
                pl.when((n & size) != 0)(functools.partial(act, zcopy(pl.multiple_of(pos, SUBLANES), size)))
                pos = pos + (n & size)
                size //= 2

        pl.loop(0, fill_n_ref.shape[0])(lambda b: fill(b, lambda c: c.start()))
        pl.loop(0, fill_n_ref.shape[0])(lambda b: fill(b, lambda c: c.wait()))


def _moe_pair_kernel(blk_ref, ea_ref, eb_ref, nvalid_ref, x_ref, gmoe_ref,
                     wga_ref, wua_ref, wda_ref, wgb_ref, wub_ref, wdb_ref, o_ref):
    i = pl.program_id(0)
    d = o_ref.shape[-1]

    @pl.when(i < nvalid_ref[0])
    def _():
        x = x_ref[...]
        h2 = x[:, 0:d]
        gates = x[:, d:]
        t = _rms(h2, gmoe_ref[...]).astype(BF16)
        lane = lax.broadcasted_iota(jnp.int32, (1, LANES), 1)

        def hidden(e, wg_ref, wu_ref):
            ge = jnp.sum(jnp.where(lane == N_GROUPS + e, gates, 0.0), axis=-1, keepdims=True)
            a = jnp.dot(t, wg_ref[0], preferred_element_type=F32)
            u = jnp.dot(t, wu_ref[0], preferred_element_type=F32)
            return (a * jax.nn.sigmoid(a) * u * ge).astype(BF16)

        ha = hidden(ea_ref[i], wga_ref, wua_ref)
        hb = hidden(eb_ref[i], wgb_ref, wub_ref)
        o_ref[...] = (h2 + jnp.dot(ha, wda_ref[0], preferred_element_type=F32)
                      + jnp.dot(hb, wdb_ref[0], preferred_element_type=F32))


def _gather_rows_kernel(tile_ref, sub_ref, y_hbm, o_ref, sem):
    n8 = o_ref.shape[0]
    base = pl.program_id(0) * n8 * SUBLANES

    @pl.loop(0, n8)
    def _(r8):
        r0 = base + r8 * SUBLANES
        for k in range(SUBLANES):
            pltpu.make_async_copy(y_hbm.at[tile_ref[r0 + k], pl.ds(sub_ref[r0 + k] & (SUBLANES - 1), 1)],
                                  o_ref.at[r8, pl.ds(k, 1)], sem).start()

    pltpu.make_async_copy(y_hbm.at[pl.ds(0, n8)], o_ref, sem).wait()


def _rope_block(x1, x2):
    z = jnp.zeros(x1.shape[:-1] + (32,), x1.dtype)
    return jnp.concatenate([x1, x1, z, x2, x2, z], axis=-1)


def _pair_rope_block(a, b):
    z = jnp.zeros(a.shape[:-1] + (32,), a.dtype)
    return jnp.concatenate([a[..., :16], b[..., :16], z, a[..., 16:], b[..., 16:], z], axis=-1)


def _row(v):
    return v.reshape(1, -1).astype(F32)


def kernel(x, mem, positions, norm_mix_g, w_in, mla_q_norm_g, w_uq, mla_kv_norm_g, w_ukv, mla_qn_g, mla_qr_g, mla_kn_g, mla_kr_g, sb_out_g, mla_out_g, w_o, norm_mem_g, mem_src_g, w_mq, w_mkv, mem_qhead_g, mem_khead_g, w_mo, norm_moe_g, w_group, b_group, w_router, b_router, w_gate, w_up, w_down):
    B, S, D = x.shape
    T = B * S
    M = mem.shape[1]
    FF = w_gate.shape[-1]
    half = MLA_ROPE // 2
    qk = MLA_NOPE + MLA_ROPE

    o = 3 * SB_WIDTH + MLA_Q_RANK + MLA_KV_RANK
    w_sb = jnp.concatenate([w_in[:, :SB_WIDTH] * (SB_HEAD_DIM ** -0.5),
                            w_in[:, SB_WIDTH:3 * SB_WIDTH]], axis=1).astype(BF16)
    w_lat = jnp.concatenate([w_in[:, 3 * SB_WIDTH:o],
                             _rope_block(w_in[:, o:o + half], w_in[:, o + half:])], axis=1).astype(BF16)
    uq = w_uq.reshape(MLA_Q_RANK, MLA_HEADS, qk)
    ukv = w_ukv.reshape(MLA_KV_RANK, MLA_HEADS, MLA_NOPE + MLA_V)
    zk = jnp.zeros((MLA_KV_RANK, LANES), F32)
    uq_cols, uk_cols = [], []
    for p in range(N_PAIRS):
        a, b = 2 * p, 2 * p + 1
        uq_cols += [uq[:, a, :MLA_NOPE], uq[:, b, :MLA_NOPE], _pair_rope_block(uq[:, a, MLA_NOPE:], uq[:, b, MLA_NOPE:])]
        uk_cols += [ukv[:, a, :MLA_NOPE], ukv[:, b, :MLA_NOPE], zk]
    wuq = jnp.concatenate(uq_cols, axis=1).astype(BF16)
    wukv = jnp.concatenate(uk_cols + [ukv[:, h, MLA_NOPE:] for h in range(MLA_HEADS)], axis=1).astype(BF16)
    gq_pair = jnp.concatenate([mla_qn_g, mla_qn_g, _pair_rope_block(mla_qr_g, mla_qr_g)])
    gk_pair = jnp.concatenate([mla_kn_g, mla_kn_g, jnp.zeros((LANES,), F32)])
    gqcat = _row(jnp.tile(gq_pair, N_PAIRS))
    gkcat = _row(jnp.tile(gk_pair, N_PAIRS))
    gkr = _row(_rope_block(mla_kr_g[:half], mla_kr_g[half:]))
    inv_freq = ROPE_THETA ** (-(jnp.arange(half, dtype=F32) * 2.0 / MLA_ROPE))
    invf = _row(_rope_block(inv_freq, inv_freq))
    sgn = _row(jnp.concatenate([-jnp.ones((64,), F32), jnp.ones((64,), F32)]))
    wr = jnp.concatenate([w_group, w_router, jnp.zeros((D, LANES - N_GROUPS - N_EXPERTS), F32)], axis=1)
    wr_hi = wr.astype(BF16)
    wr_cat = jnp.concatenate([wr_hi, (wr - wr_hi.astype(F32)).astype(BF16)], axis=1)
    br = _row(jnp.concatenate([b_group, b_router, jnp.zeros((LANES - N_GROUPS - N_EXPERTS,), F32)]))

    full = lambda shape: pl.BlockSpec(shape, lambda *_: (0,) * len(shape))

    mk, mv = pl.pallas_call(
        _mem_kv_kernel, grid=(B,), name="mem_kv",
        in_specs=[pl.BlockSpec((1, M, D), lambda b: (b, 0, 0)), full((1, D)), full((D, 2 * D)),
                  full((1, D // MEM_HEADS))],
        out_specs=[pl.BlockSpec((1, M, D), lambda b: (b, 0, 0))] * 2,
        out_shape=[jax.ShapeDtypeStruct((B, M, D), BF16)] * 2,
    )(mem, _row(mem_src_g), w_mkv.astype(BF16), _row(mem_khead_g))

    tm = 512
    rows = lambda w: pl.BlockSpec((tm, w), lambda i: (i, 0))
    sbq, sbk, sbv, mq, mkk, mvv = pl.pallas_call(
        _in_proj_kernel, grid=(T // tm,), name="in_proj",
        in_specs=[rows(D), rows(1), full((1, D)), full(w_lat.shape), full(w_sb.shape),
                  full((1, MLA_Q_RANK)), full(wuq.shape),
                  full((1, MLA_KV_RANK)), full(wukv.shape), full(gqcat.shape), full(gkcat.shape),
                  full((1, LANES)), full((1, LANES)), full((1, LANES))],
        out_specs=[rows(SB_WIDTH)] * 3 + [rows(N_PAIRS * PAIR_W)] * 2
                  + [pl.BlockSpec((MLA_WIDTH, tm), lambda i: (0, i))],
        out_shape=[jax.ShapeDtypeStruct((T, SB_WIDTH), BF16)] * 3
                  + [jax.ShapeDtypeStruct((T, N_PAIRS * PAIR_W), BF16)] * 2
                  + [jax.ShapeDtypeStruct((MLA_WIDTH, T), BF16)],
        scratch_shapes=[pltpu.VMEM((tm, LANES), F32)] * 2,
    )(x.reshape(T, D), positions.reshape(T, 1), _row(norm_mix_g), w_lat, w_sb, _row(mla_q_norm_g), wuq,
      _row(mla_kv_norm_g), wukv, gqcat, gkcat, gkr, invf, sgn)

    qspec = lambda t, w: pl.BlockSpec((1, t, w), lambda b, p, i: (b, i, p))
    kvspec = lambda w: pl.BlockSpec((1, S, w), lambda b, p, i: (b, 0, p))
    tq = 256
    const3 = lambda shape: pl.BlockSpec(shape, lambda b, p, i, *_: (0,) * len(shape))
    sb = pl.pallas_call(
        functools.partial(_sb_attn_kernel, tq=tq), grid=(B, SB_HEADS // 2, S // tq), name="sb_attn",
        in_specs=[qspec(tq, LANES), kvspec(LANES), kvspec(LANES)],
        out_specs=qspec(tq, LANES),
        out_shape=jax.ShapeDtypeStruct((B, S, SB_WIDTH), BF16),
    )(sbq.reshape(B, S, SB_WIDTH), sbk.reshape(B, S, SB_WIDTH), sbv.reshape(B, S, SB_WIDTH))
    tqm = 512
    gmax2 = lambda g: jnp.max(jnp.square(g))
    q_bound = jnp.sqrt(MLA_NOPE * gmax2(mla_qn_g) + MLA_ROPE * gmax2(mla_qr_g))
    k_bound = jnp.sqrt(MLA_NOPE * gmax2(mla_kn_g) + MLA_ROPE * gmax2(mla_kr_g))
    logit_bound = 1.02 * q_bound * k_bound * (qk ** -0.5) * LOG2E
    bounded = (logit_bound < MLA_SAFE_LOGIT).astype(jnp.int32).reshape(1)
    key_c = jnp.arange(tqm)[:, None] // CHUNK
    qry_c = (jnp.arange(2 * tqm)[None, :] % tqm) // CHUNK
    mla_diag = jnp.where(key_c <= qry_c, 0.0, -jnp.inf).astype(F32)
    mla = pl.pallas_call(
        functools.partial(_mla_attn_kernel, tq=tqm), name="mla_attn",
        grid_spec=pltpu.PrefetchScalarGridSpec(
            num_scalar_prefetch=1, grid=(B, N_PAIRS, S // tqm),
            in_specs=[pl.BlockSpec((1, tqm, PAIR_W), lambda b, p, i, *_: (b, i, p)),
                      pl.BlockSpec((1, S, PAIR_W), lambda b, p, i, *_: (b, 0, p)),
                      pl.BlockSpec((LANES, S), lambda b, p, i, *_: (p, b)),
                      const3((tqm, 2 * tqm))],
            out_specs=pl.BlockSpec((1, tqm, LANES), lambda b, p, i, *_: (b, i, p)),
            scratch_shapes=[pltpu.VMEM((1, 2 * tqm), F32), pltpu.VMEM((LANES, tqm), F32)]),
        out_shape=jax.ShapeDtypeStruct((B, S, MLA_WIDTH), BF16),
    )(bounded, mq.reshape(B, S, -1), mkk.reshape(B, S, -1), mvv, mla_diag)

    tp = 1024
    prow = lambda w: pl.BlockSpec((tp, w), lambda i: (i, 0))
    memspec = pl.BlockSpec((1, M, D), lambda i: ((i * tp) // S, 0, 0))
    W = D + LANES
    h2g, dest, cnt = pl.pallas_call(
        functools.partial(_post_attn_kernel, n_tokens=T), grid=(T // tp,), name="post_attn",
        in_specs=[prow(D), prow(SB_WIDTH), prow(MLA_WIDTH), full((1, SB_WIDTH)), full((1, MLA_WIDTH)),
                  full((D, D)), full((1, D)), full((D, D)), full((1, D // MEM_HEADS)), memspec, memspec,
                  full((D, D)), full((1, D)), full((D, 2 * LANES)), full((1, LANES))],
        out_specs=[prow(W), prow(1), pl.BlockSpec((1, 1, LANES), lambda i: (i, 0, 0))],
        out_shape=[jax.ShapeDtypeStruct((T, W), F32), jax.ShapeDtypeStruct((T, 1), jnp.int32),
                   jax.ShapeDtypeStruct((T // tp, 1, LANES), F32)],
        scratch_shapes=[pltpu.VMEM((1, LANES), F32)],
        compiler_params=pltpu.CompilerParams(dimension_semantics=("arbitrary",)),
    )(x.reshape(T, D), sb.reshape(T, SB_WIDTH), mla.reshape(T, MLA_WIDTH), _row(sb_out_g), _row(mla_out_g),
      w_o.astype(BF16), _row(norm_mem_g), w_mq.astype(BF16), _row(mem_qhead_g), mk, mv,
      w_mo.astype(BF16), _row(norm_moe_g), wr_cat, br)

    te = 256
    tg = 2048
    n_buckets = N_GROUPS * PAIRS_PER_GROUP
    n_steps = T // te + n_buckets
    counts = cnt[-1, 0, :n_buckets].astype(jnp.int32)
    tiles = (counts + te - 1) // te
    ends = jnp.cumsum(tiles)
    first_row = (ends - tiles) * te
    n_valid = ends[-1]
    slot = dest.reshape(T // LANES, LANES)
    slot_bucket = slot // T
    row0 = jnp.zeros_like(slot)
    for b in range(n_buckets):
        row0 = jnp.where(slot_bucket == b, first_row[b], row0)
    dest = (row0 + slot % T).reshape(T)
    step = jnp.minimum(jnp.arange(n_steps, dtype=jnp.int32), n_valid - 1)
    bucket = jnp.zeros_like(step)
    for b in range(n_buckets):
        bucket = bucket + (step >= ends[b]).astype(jnp.int32)
    pr = bucket % PAIRS_PER_GROUP
    pair_a = (pr >= 3).astype(jnp.int32) + (pr >= 5).astype(jnp.int32)
    pair_b = pr + 1 - 2 * (pr >= 3).astype(jnp.int32) - (pr >= 5).astype(jnp.int32)
    ea = bucket // PAIRS_PER_GROUP * EXPERTS_PER_GROUP + pair_a
    eb = bucket // PAIRS_PER_GROUP * EXPERTS_PER_GROUP + pair_b

    any_spec = pl.BlockSpec(memory_space=pl.ANY)
    h2s = pl.pallas_call(
        _scatter_rows_kernel, name="moe_dispatch",
        grid_spec=pltpu.PrefetchScalarGridSpec(
            num_scalar_prefetch=3, grid=(T // tg,),
            in_specs=[pl.BlockSpec((tg, W), lambda i, *_: (i, 0))],
            out_specs=any_spec,
            scratch_shapes=[pltpu.VMEM((te // 2, W), F32),
                            pltpu.SemaphoreType.DMA, pltpu.SemaphoreType.DMA]),
        out_shape=jax.ShapeDtypeStruct((n_steps * te, W), F32),
        compiler_params=pltpu.CompilerParams(dimension_semantics=("arbitrary",)),
    )(dest, first_row + counts, tiles * te - counts, h2g)

    wspec = lambda shape, which: pl.BlockSpec((1,) + shape, lambda i, blk, ea, eb, nv: ((ea, eb)[which][i], 0, 0))
    w_gate_b, w_up_b, w_down_b = w_gate.astype(BF16), w_up.astype(BF16), w_down.astype(BF16)
    ys = pl.pallas_call(
        _moe_pair_kernel, name="moe_experts",
        grid_spec=pltpu.PrefetchScalarGridSpec(
            num_scalar_prefetch=4, grid=(n_steps,),
            in_specs=[pl.BlockSpec((te, W), lambda i, blk, *_: (blk[i], 0)),
                      pl.BlockSpec((1, D), lambda i, *_: (0, 0)),
                      wspec((D, FF), 0), wspec((D, FF), 0), wspec((FF, D), 0),
                      wspec((D, FF), 1), wspec((D, FF), 1), wspec((FF, D), 1)],
            out_specs=pl.BlockSpec((te, D), lambda i, blk, *_: (blk[i], 0))),
        out_shape=jax.ShapeDtypeStruct((n_steps * te, D), F32),
        compiler_params=pltpu.CompilerParams(dimension_semantics=("arbitrary",)),
    )(step, ea, eb, n_valid.reshape(1), h2s, _row(norm_moe_g),
      w_gate_b, w_up_b, w_down_b, w_gate_b, w_up_b, w_down_b)

    out = pl.pallas_call(
        _gather_rows_kernel, name="moe_combine",
        grid_spec=pltpu.PrefetchScalarGridSpec(
            num_scalar_prefetch=2, grid=(T // tg,),
            in_specs=[any_spec],
            out_specs=pl.BlockSpec((tg // SUBLANES, SUBLANES, D), lambda i, *_: (i, 0, 0)),
            scratch_shapes=[pltpu.SemaphoreType.DMA]),
        out_shape=jax.ShapeDtypeStruct((T // SUBLANES, SUBLANES, D), F32),
        compiler_params=pltpu.CompilerParams(dimension_semantics=("arbitrary",)),
    )(dest // SUBLANES, dest % SUBLANES, ys.reshape(-1, SUBLANES, D))
    return out.reshape(B, S, D)
```

```python
import functools

import numpy as np
import jax
import jax.numpy as jnp
from jax import lax
from jax.experimental import pallas as pl
from jax.experimental.pallas import tpu as pltpu

F32 = jnp.float32
BF16 = jnp.bfloat16
EPS = 1e-6
LANES = 128
SUBLANES = 8

CHUNK = 64
SB_HEADS = 8
SB_HEAD_DIM = 64
SB_WIDTH = SB_HEADS * SB_HEAD_DIM
MLA_HEADS = 8
MLA_NOPE = 64
MLA_ROPE = 32
MLA_V = 64
MLA_WIDTH = MLA_HEADS * MLA_V
MLA_Q_RANK = 384
MLA_KV_RANK = 256
ROPE_THETA = 10000.0
MEM_HEADS = 4
N_GROUPS = 4
EXPERTS_PER_GROUP = 4
N_EXPERTS = N_GROUPS * EXPERTS_PER_GROUP
PAIRS_PER_GROUP = EXPERTS_PER_GROUP * (EXPERTS_PER_GROUP - 1) // 2
RANK_BLOCK = 256
POST_ATTN_CHUNKS = 2
N_PAIRS = MLA_HEADS // 2
PAIR_W = 2 * LANES
LOG2E = 1.4426950408889634
SB_EXP2_UNDERFLOW = 150.0
SB_DROPPED = -1e30
MLA_SAFE_LOGIT = 60.0

_TRANS_B = (((1,), (1,)), ((), ()))


def _rms(x, g):
    return x * lax.rsqrt(jnp.mean(x * x, axis=-1, keepdims=True) + EPS) * g


def _seg_rms(x, masks, seg_len, g):
    x2 = x * x
    r = jnp.zeros_like(x)
    for m in masks:
        s = jnp.sum(jnp.where(m, x2, 0.0), axis=-1, keepdims=True)
        r = jnp.where(m, lax.rsqrt(s * (1.0 / seg_len) + EPS), r)
    return x * r * g


def _mem_kv_kernel(mem_ref, gsrc_ref, w_ref, gk_ref, mk_ref, mv_ref):
    d = mem_ref.shape[-1]
    hd = d // MEM_HEADS
    mn = _rms(mem_ref[0], gsrc_ref[...]).astype(BF16)
    kv = jnp.dot(mn, w_ref[...], preferred_element_type=F32)
    for h in range(MEM_HEADS):
        kh = kv[:, h * hd:(h + 1) * hd]
        mk_ref[0, :, h * hd:(h + 1) * hd] = _rms(kh, gk_ref[...]).astype(BF16)
    mv_ref[0] = kv[:, d:].astype(BF16)


def _in_proj_kernel(x_ref, pos_ref, gmix_ref, wlat_ref, wsb_ref, gq_ref, wuq_ref, gkv_ref, wukv_ref,
                    gqcat_ref, gkcat_ref, gkr_ref, invf_ref, sgn_ref,
                    sbq_ref, sbk_ref, sbv_ref, mq_ref, mk_ref, mv_ref, cos_ref, sin_ref):
    ang = pos_ref[...].astype(F32) * invf_ref[...]
    cos_ref[...] = jnp.cos(ang)
    sin_ref[...] = jnp.sin(ang) * sgn_ref[...]

    hn = _rms(x_ref[...], gmix_ref[...]).astype(BF16)
    lat = jnp.dot(hn, wlat_ref[...], preferred_element_type=F32)
    cq = lat[:, 0:MLA_Q_RANK]
    ckv = lat[:, MLA_Q_RANK:MLA_Q_RANK + MLA_KV_RANK]
    kr = lat[:, MLA_Q_RANK + MLA_KV_RANK:]
    q = jnp.dot(_rms(cq, gq_ref[...]).astype(BF16), wuq_ref[...], preferred_element_type=F32)
    kv = jnp.dot(_rms(ckv, gkv_ref[...]).astype(BF16), wukv_ref[...], preferred_element_type=F32)

    sb = jnp.dot(hn, wsb_ref[...], preferred_element_type=F32)
    sbq_ref[...] = sb[:, 0:SB_WIDTH].astype(BF16)
    sbk_ref[...] = sb[:, SB_WIDTH:2 * SB_WIDTH].astype(BF16)
    sbv_ref[...] = sb[:, 2 * SB_WIDTH:3 * SB_WIDTH].astype(BF16)

    lane = lax.broadcasted_iota(jnp.int32, (1, LANES), 1)
    lr = lane & 63
    lo, hi = lane < 64, lane >= 64
    r0, r1 = lr < 16, (lr >= 16) & (lr < 32)

    def rope(b):
        return b * cos_ref[...] + pltpu.roll(b, 64, 1) * sin_ref[...]

    q_scale = (MLA_NOPE + MLA_ROPE) ** -0.5 * LOG2E
    krn = rope(_seg_rms(kr, (r0, r1), MLA_ROPE, gkr_ref[...]))
    for p in range(N_PAIRS):
        b0 = p * PAIR_W
        qa = _seg_rms(q[:, b0:b0 + LANES], (lo, hi), MLA_NOPE, gqcat_ref[:, b0:b0 + LANES])
        qb = rope(_seg_rms(q[:, b0 + LANES:b0 + PAIR_W], (r0, r1), MLA_ROPE,
                           gqcat_ref[:, b0 + LANES:b0 + PAIR_W]))
        mq_ref[:, b0:b0 + LANES] = (qa * q_scale).astype(BF16)
        mq_ref[:, b0 + LANES:b0 + PAIR_W] = (qb * q_scale).astype(BF16)
        ka = _seg_rms(kv[:, b0:b0 + LANES], (lo, hi), MLA_NOPE, gkcat_ref[:, b0:b0 + LANES])
        mk_ref[:, b0:b0 + LANES] = ka.astype(BF16)
        mk_ref[:, b0 + LANES:b0 + PAIR_W] = krn.astype(BF16)
    mv_ref[...] = kv[:, N_PAIRS * PAIR_W:].T.astype(BF16)


def _sb_attn_kernel(q_ref, k_ref, v_ref, o_ref, *, tq):
    i = pl.program_id(2)
    q = q_ref[0]
    lane = lax.broadcasted_iota(jnp.int32, (1, LANES), 1)
    lo = lane < 64
    zq = jnp.zeros_like(q)
    qcat = jnp.concatenate([jnp.where(lo, q, zq), jnp.where(lo, zq, q)], axis=0)
    row = lax.broadcasted_iota(jnp.int32, (tq, tq), 0)
    col = lax.broadcasted_iota(jnp.int32, (tq, tq), 1)
    later = (row > col).astype(BF16)
    past = jnp.concatenate([col < row, col < row], axis=0)

    def scores(j, mask=None):
        start = pl.multiple_of(j * tq, tq)
        z = lax.dot_general(qcat, k_ref[0, pl.ds(start, tq), :], _TRANS_B, preferred_element_type=F32)
        if mask is not None:
            z = mask(z)
        return z, jnp.maximum(z, 0.0) + jnp.log2(1.0 + jnp.exp2(-jnp.abs(z)))

    def weights(z, fail, after):
        between = after + jnp.dot(fail.astype(BF16), later, preferred_element_type=F32)
        return jnp.exp2((z - fail) - between).astype(BF16)

    def pv(ws, js):
        zv = jnp.zeros((tq, LANES), BF16)
        vs = [v_ref[0, pl.ds(pl.multiple_of(j * tq, tq), tq), :] for j in js]
        wcat = jnp.concatenate([w[:tq] for w in ws] + [w[tq:] for w in ws], axis=1)
        vcat = jnp.concatenate([jnp.where(lo, v, zv) for v in vs] + [jnp.where(lo, zv, v) for v in vs], axis=0)
        return jnp.dot(wcat, vcat, preferred_element_type=F32)

    prev = jnp.maximum(i - 1, 0)
    zb, fb = scores(i, lambda z: jnp.where(past, z, SB_DROPPED))
    za, fa = scores(prev, lambda z: jnp.where(i > 0, z, SB_DROPPED))
    tot_b = jnp.sum(fb, axis=1, keepdims=True)
    wb = weights(zb, fb, 0.0)
    wa = weights(za, fa, tot_b)
    carry0 = tot_b + jnp.sum(fa, axis=1, keepdims=True)
    acc0 = pv([wa, wb], [prev, i])

    def cond(st):
        j, m, _, _ = st
        return (j >= 0) & (m < SB_EXP2_UNDERFLOW)

    def body(st):
        j, _, carry, acc = st
        z, fail = scores(j)
        w = weights(z, fail, carry)
        carry = carry + jnp.sum(fail, axis=1, keepdims=True)
        return j - 1, jnp.min(carry), carry, acc + pv([w], [j])

    _, _, _, acc = lax.while_loop(cond, body, (i - 2, jnp.min(carry0), carry0, acc0))
    o_ref[0] = acc.astype(o_ref.dtype)


def _mla_attn_kernel(bounded_ref, q_ref, k_ref, v_ref, diag_ref, o_ref, l_ref, acc_ref, *, tq):
    i = pl.program_id(2)
    q = q_ref[0]
    l2 = lax.broadcasted_iota(jnp.int32, (1, PAIR_W), 1)
    lr = l2 & 63
    hm0 = (l2 < 64) | ((l2 >= LANES) & (lr < 16))
    hm1 = ((l2 >= 64) & (l2 < LANES)) | ((l2 >= LANES) & (lr >= 16) & (lr < 32))
    zq = jnp.zeros_like(q)
    qcat = jnp.concatenate([jnp.where(hm0, q, zq), jnp.where(hm1, q, zq)], axis=0)
    head0 = lax.broadcasted_iota(jnp.int32, (LANES, 1), 0) < 64

    def per_head(x):
        return jnp.where(head0, x[:, :tq], x[:, tq:])

    def logits(j, n, diag):
        start = pl.multiple_of(j * tq, tq)
        s = lax.dot_general(k_ref[0, pl.ds(start, n * tq), :], qcat, _TRANS_B, preferred_element_type=F32)
        if diag:
            tiles = [s[t * tq:(t + 1) * tq] for t in range(n)]
            s = jnp.concatenate(tiles[:-1] + [tiles[-1] + diag_ref[...]], axis=0)
        return s

    def pv(j, n, p):
        vt = v_ref[:, pl.ds(pl.multiple_of(j * tq, tq), n * tq)]
        pb = p.astype(BF16)
        zv = jnp.zeros_like(vt)
        return jnp.dot(jnp.concatenate([jnp.where(head0, vt, zv), jnp.where(head0, zv, vt)], axis=1),
                       jnp.concatenate([pb[:, :tq], pb[:, tq:]], axis=0),
                       preferred_element_type=F32)

    def finish(l, acc):
        o_ref[0] = (acc / per_head(l)).T.astype(o_ref.dtype)

    @pl.when(bounded_ref[0] != 0)
    def _():
        def step(j, n, diag):
            p = jnp.exp2(logits(j, n, diag))
            l_ref[...] += jnp.sum(p, axis=0, keepdims=True)
            acc_ref[...] += pv(j, n, p)

        l_ref[...] = jnp.zeros_like(l_ref)
        acc_ref[...] = jnp.zeros_like(acc_ref)

        @pl.loop(0, i // 4)
        def _(t):
            step(4 * t, 4, False)

        @pl.when(i % 4 >= 2)
        def _():
            step(i // 4 * 4, 2, False)

        @pl.when(i % 2 == 1)
        def _():
            step(i - 1, 2, True)

        @pl.when(i % 2 == 0)
        def _():
            step(i, 1, True)

        finish(l_ref[...], acc_ref[...])

    @pl.when(bounded_ref[0] == 0)
    def _():
        def step(j, st, diag):
            m, l, acc = st
            s = logits(j, 1, diag)
            m_new = jnp.maximum(m, jnp.max(s, axis=0, keepdims=True))
            alpha = jnp.exp2(m - m_new)
            p = jnp.exp2(s - m_new)
            l_new = alpha * l + jnp.sum(p, axis=0, keepdims=True)
            return m_new, l_new, per_head(alpha) * acc + pv(j, 1, p)

        st = step(i, (jnp.full((1, 2 * tq), -jnp.inf, F32), jnp.zeros((1, 2 * tq), F32),
                      jnp.zeros((LANES, tq), F32)), True)
        _, l, acc = lax.fori_loop(0, i, lambda j, st: step(j, st, False), st)
        finish(l, acc)


def _post_attn_kernel(x_ref, sb_ref, mla_ref, gsb_ref, gmla_ref, wo_ref, gmem_ref, wmq_ref, gqh_ref,
                      mk_ref, mv_ref, wmo_ref, gmoe_ref, wr_ref, br_ref,
                      h2g_ref, dest_ref, cnt_ref, carry_ref, *, n_tokens):
    d = x_ref.shape[-1]
    hd = d // MEM_HEADS
    tp = x_ref.shape[0]
    lane = lax.broadcasted_iota(jnp.int32, (1, LANES), 1).astype(F32)

    n_chunks = POST_ATTN_CHUNKS
    chunks = [pl.ds(c * (tp // n_chunks), tp // n_chunks) for c in range(n_chunks)]
    each = lambda f, *lists: [f(*args) for args in zip(*lists)]

    mixed = each(lambda r: jnp.concatenate(
        [_rms(sb_ref[r, :].astype(F32), gsb_ref[...]).astype(BF16),
         _rms(mla_ref[r, :].astype(F32), gmla_ref[...]).astype(BF16)], axis=1), chunks)
    h1 = each(lambda r, m: x_ref[r, :] + jnp.dot(m, wo_ref[...], preferred_element_type=F32), chunks, mixed)
    hq = each(lambda h: _rms(h, gmem_ref[...]).astype(BF16), h1)
    mq = each(lambda h: jnp.dot(h, wmq_ref[...], preferred_element_type=F32), hq)

    def mem_head(m, h):
        qh = (_rms(m[:, h * hd:(h + 1) * hd], gqh_ref[...]) * (hd ** -0.5)).astype(BF16)
        sc = lax.dot_general(qh, mk_ref[0, :, h * hd:(h + 1) * hd], _TRANS_B, preferred_element_type=F32)
        sc = sc - jnp.max(sc, axis=-1, keepdims=True)
        e = jnp.exp(sc)
        p = (e / jnp.sum(e, axis=-1, keepdims=True)).astype(BF16)
        return jnp.dot(p, mv_ref[0, :, h * hd:(h + 1) * hd], preferred_element_type=F32).astype(BF16)

    mos = [each(lambda m, h=h: mem_head(m, h), mq) for h in range(MEM_HEADS)]
    h2 = each(lambda h, *mo: h + jnp.dot(jnp.concatenate(mo, axis=1), wmo_ref[...],
                                         preferred_element_type=F32), h1, *mos)

    def route(rows, h2c):
        h2g_ref[rows, 0:d] = h2c
        t = _rms(h2c, gmoe_ref[...])
        t_hi = t.astype(BF16)
        t_lo = (t - t_hi.astype(F32)).astype(BF16)
        both = jnp.dot(t_hi, wr_ref[...], preferred_element_type=F32)
        logits = (both[:, :LANES] + both[:, LANES:]
                  + jnp.dot(t_lo, wr_ref[:, :LANES], preferred_element_type=F32)) + br_ref[...]

        big = float(LANES)
        neg = -jnp.inf
        lg = jnp.where(lane < N_GROUPS, logits, neg)
        gmax = jnp.max(lg, axis=-1, keepdims=True)
        g_idx = jnp.min(jnp.where(lg == gmax, lane, big), axis=-1, keepdims=True)
        g_w = 1.0 / jnp.sum(jnp.exp(lg - gmax), axis=-1, keepdims=True)
        e_lo = N_GROUPS + g_idx * EXPERTS_PER_GROUP
        in_grp = (lane >= e_lo) & (lane < e_lo + EXPERTS_PER_GROUP)
        le = jnp.where(in_grp, logits, neg)
        m1 = jnp.max(le, axis=-1, keepdims=True)
        i1 = jnp.min(jnp.where(le == m1, lane, big), axis=-1, keepdims=True)
        le2 = jnp.where(lane == i1, neg, le)
        m2 = jnp.max(le2, axis=-1, keepdims=True)
        i2 = jnp.min(jnp.where(le2 == m2, lane, big), axis=-1, keepdims=True)
        e2 = jnp.exp(m2 - m1)
        w1 = g_w / (1.0 + e2)
        w2 = g_w * e2 / (1.0 + e2)
        h2g_ref[rows, d:] = jnp.where(lane == i1, w1, 0.0) + jnp.where(lane == i2, w2, 0.0)

        ka = jnp.minimum(i1, i2) - e_lo
        kb = jnp.maximum(i1, i2) - e_lo
        pair = ka * (5.0 - ka) * 0.5 + kb - 1.0
        return g_idx * float(PAIRS_PER_GROUP) + pair

    bucket = jnp.concatenate(each(route, chunks, h2), axis=0)

    @pl.when(pl.program_id(0) == 0)
    def _():
        carry_ref[...] = jnp.zeros_like(carry_ref)

    blk = RANK_BLOCK
    earlier = (lax.broadcasted_iota(jnp.int32, (blk, blk), 1)
               < lax.broadcasted_iota(jnp.int32, (blk, blk), 0)).astype(BF16)
    carry = carry_ref[...]
    for r0 in range(0, tp, blk):
        bkt = bucket[r0:r0 + blk]
        oh = jnp.where(lane == bkt, 1.0, 0.0)
        prefix = jnp.dot(earlier, oh.astype(BF16), preferred_element_type=F32)
        rank = jnp.sum(jnp.where(lane == bkt, prefix + carry, 0.0), axis=-1, keepdims=True)
        dest_ref[r0:r0 + blk, :] = (bkt * float(n_tokens) + rank).astype(jnp.int32)
        carry = carry + jnp.sum(oh, axis=0, keepdims=True)
    carry_ref[...] = carry
    cnt_ref[0] = carry


def _scatter_rows_kernel(dest_ref, fill_start_ref, fill_n_ref, x_ref, o_hbm, zrow_ref, sem, zsem):
    i = pl.program_id(0)
    tg = x_ref.shape[0]
    base = i * tg

    @pl.loop(0, tg // SUBLANES)
    def _(r8):
        r0 = pl.multiple_of(r8 * SUBLANES, SUBLANES)
        for k in range(SUBLANES):
            pltpu.make_async_copy(x_ref.at[pl.ds(r0 + k, 1)],
                                  o_hbm.at[pl.ds(dest_ref[base + r0 + k], 1)], sem).start()

    pltpu.make_async_copy(x_ref, o_hbm.at[pl.ds(0, tg)], sem).wait()

    @pl.when(i == pl.num_programs(0) - 1)
    def _():
        zrow_ref[...] = jnp.zeros_like(zrow_ref)

        def fill(b, act):
            n = fill_n_ref[b]
            pos = fill_start_ref[b]
            head = jnp.minimum((-pos) & (SUBLANES - 1), n)
            zcopy = lambda at, size: pltpu.make_async_copy(zrow_ref.at[pl.ds(0, size)],
                                                           o_hbm.at[pl.ds(at, size)], zsem)
            for k in range(SUBLANES - 1):
                pl.when(k < head)(functools.partial(act, zcopy(pos + k, 1)))
            pos = pos + head
            n = n - head
            size = zrow_ref.shape[0]
            while size >= SUBLANES:
                pl.when((n & size) != 0)(functools.partial(act, zcopy(pl.multiple_of(pos, SUBLANES), size)))
                pos = pos + (n & size)
                size //= 2

        pl.loop(0, fill_n_ref.shape[0])(lambda b: fill(b, lambda c: c.start()))
        pl.loop(0, fill_n_ref.shape[0])(lambda b: fill(b, lambda c: c.wait()))


def _moe_pair_kernel(blk_ref, ea_ref, eb_ref, nvalid_ref, x_ref, gmoe_ref,
                     wga_ref, wua_ref, wda_ref, wgb_ref, wub_ref, wdb_ref, o_ref):
    i = pl.program_id(0)
    d = o_ref.shape[-1]

    @pl.when(i < nvalid_ref[0])
    def _():
        x = x_ref[...]
        h2 = x[:, 0:d]
        gates = x[:, d:]
        t = _rms(h2, gmoe_ref[...]).astype(BF16)
        lane = lax.broadcasted_iota(jnp.int32, (1, LANES), 1)

        def hidden(e, wg_ref, wu_ref):
            ge = jnp.sum(jnp.where(lane == N_GROUPS + e, gates, 0.0), axis=-1, keepdims=True)
            a = jnp.dot(t, wg_ref[0], preferred_element_type=F32)
            u = jnp.dot(t, wu_ref[0], preferred_element_type=F32)
            return (a * jax.nn.sigmoid(a) * u * ge).astype(BF16)

        ha = hidden(ea_ref[i], wga_ref, wua_ref)
        hb = hidden(eb_ref[i], wgb_ref, wub_ref)
        o_ref[...] = (h2 + jnp.dot(ha, wda_ref[0], preferred_element_type=F32)
                      + jnp.dot(hb, wdb_ref[0], preferred_element_type=F32))


def _gather_rows_kernel(tile_ref, sub_ref, y_hbm, o_ref, sem):
    n8 = o_ref.shape[0]
    base = pl.program_id(0) * n8 * SUBLANES

    @pl.loop(0, n8)
    def _(r8):
        r0 = base + r8 * SUBLANES
        for k in range(SUBLANES):
            pltpu.make_async_copy(y_hbm.at[tile_ref[r0 + k], pl.ds(sub_ref[r0 + k] & (SUBLANES - 1), 1)],
                                  o_ref.at[r8, pl.ds(k, 1)], sem).start()

    pltpu.make_async_copy(y_hbm.at[pl.ds(0, n8)], o_ref, sem).wait()


def _rope_block(x1, x2):
    z = jnp.zeros(x1.shape[:-1] + (32,), x1.dtype)
    return jnp.concatenate([x1, x1, z, x2, x2, z], axis=-1)


def _pair_rope_block(a, b):
    z = jnp.zeros(a.shape[:-1] + (32,), a.dtype)
    return jnp.concatenate([a[..., :16], b[..., :16], z, a[..., 16:], b[..., 16:], z], axis=-1)


def _row(v):
    return v.reshape(1, -1).astype(F32)


def kernel(x, mem, positions, norm_mix_g, w_in, mla_q_norm_g, w_uq, mla_kv_norm_g, w_ukv, mla_qn_g, mla_qr_g, mla_kn_g, mla_kr_g, sb_out_g, mla_out_g, w_o, norm_mem_g, mem_src_g, w_mq, w_mkv, mem_qhead_g, mem_khead_g, w_mo, norm_moe_g, w_group, b_group, w_router, b_router, w_gate, w_up, w_down):
    B, S, D = x.shape
    T = B * S
    M = mem.shape[1]
    FF = w_gate.shape[-1]
    half = MLA_ROPE // 2
    qk = MLA_NOPE + MLA_ROPE

    o = 3 * SB_WIDTH + MLA_Q_RANK + MLA_KV_RANK
    w_sb = jnp.concatenate([w_in[:, :SB_WIDTH] * (SB_HEAD_DIM ** -0.5 * LOG2E),
                            w_in[:, SB_WIDTH:3 * SB_WIDTH]], axis=1).astype(BF16)
    w_lat = jnp.concatenate([w_in[:, 3 * SB_WIDTH:o],
                             _rope_block(w_in[:, o:o + half], w_in[:, o + half:])], axis=1).astype(BF16)
    uq = w_uq.reshape(MLA_Q_RANK, MLA_HEADS, qk)
    ukv = w_ukv.reshape(MLA_KV_RANK, MLA_HEADS, MLA_NOPE + MLA_V)
    zk = jnp.zeros((MLA_KV_RANK, LANES), F32)
    uq_cols, uk_cols = [], []
    for p in range(N_PAIRS):
        a, b = 2 * p, 2 * p + 1
        uq_cols += [uq[:, a, :MLA_NOPE], uq[:, b, :MLA_NOPE], _pair_rope_block(uq[:, a, MLA_NOPE:], uq[:, b, MLA_NOPE:])]
        uk_cols += [ukv[:, a, :MLA_NOPE], ukv[:, b, :MLA_NOPE], zk]
    wuq = jnp.concatenate(uq_cols, axis=1).astype(BF16)
    wukv = jnp.concatenate(uk_cols + [ukv[:, h, MLA_NOPE:] for h in range(MLA_HEADS)], axis=1).astype(BF16)
    gq_pair = jnp.concatenate([mla_qn_g, mla_qn_g, _pair_rope_block(mla_qr_g, mla_qr_g)])
    gk_pair = jnp.concatenate([mla_kn_g, mla_kn_g, jnp.zeros((LANES,), F32)])
    gqcat = _row(jnp.tile(gq_pair, N_PAIRS))
    gkcat = _row(jnp.tile(gk_pair, N_PAIRS))
    gkr = _row(_rope_block(mla_kr_g[:half], mla_kr_g[half:]))
    inv_freq = ROPE_THETA ** (-(jnp.arange(half, dtype=F32) * 2.0 / MLA_ROPE))
    invf = _row(_rope_block(inv_freq, inv_freq))
    sgn = _row(jnp.concatenate([-jnp.ones((64,), F32), jnp.ones((64,), F32)]))
    wr = jnp.concatenate([w_group, w_router, jnp.zeros((D, LANES - N_GROUPS - N_EXPERTS), F32)], axis=1)
    wr_hi = wr.astype(BF16)
    wr_cat = jnp.concatenate([wr_hi, (wr - wr_hi.astype(F32)).astype(BF16)], axis=1)
    br = _row(jnp.concatenate([b_group, b_router, jnp.zeros((LANES - N_GROUPS - N_EXPERTS,), F32)]))

    full = lambda shape: pl.BlockSpec(shape, lambda *_: (0,) * len(shape))

    mk, mv = pl.pallas_call(
        _mem_kv_kernel, grid=(B,), name="mem_kv",
        in_specs=[pl.BlockSpec((1, M, D), lambda b: (b, 0, 0)), full((1, D)), full((D, 2 * D)),
                  full((1, D // MEM_HEADS))],
        out_specs=[pl.BlockSpec((1, M, D), lambda b: (b, 0, 0))] * 2,
        out_shape=[jax.ShapeDtypeStruct((B, M, D), BF16)] * 2,
    )(mem, _row(mem_src_g), w_mkv.astype(BF16), _row(mem_khead_g))

    tm = 512
    rows = lambda w: pl.BlockSpec((tm, w), lambda i: (i, 0))
    sbq, sbk, sbv, mq, mkk, mvv = pl.pallas_call(
        _in_proj_kernel, grid=(T // tm,), name="in_proj",
        in_specs=[rows(D), rows(1), full((1, D)), full(w_lat.shape), full(w_sb.shape),
                  full((1, MLA_Q_RANK)), full(wuq.shape),
                  full((1, MLA_KV_RANK)), full(wukv.shape), full(gqcat.shape), full(gkcat.shape),
                  full((1, LANES)), full((1, LANES)), full((1, LANES))],
        out_specs=[rows(SB_WIDTH)] * 3 + [rows(N_PAIRS * PAIR_W)] * 2
                  + [pl.BlockSpec((MLA_WIDTH, tm), lambda i: (0, i))],
        out_shape=[jax.ShapeDtypeStruct((T, SB_WIDTH), BF16)] * 3
                  + [jax.ShapeDtypeStruct((T, N_PAIRS * PAIR_W), BF16)] * 2
                  + [jax.ShapeDtypeStruct((MLA_WIDTH, T), BF16)],
        scratch_shapes=[pltpu.VMEM((tm, LANES), F32)] * 2,
    )(x.reshape(T, D), positions.reshape(T, 1), _row(norm_mix_g), w_lat, w_sb, _row(mla_q_norm_g), wuq,
      _row(mla_kv_norm_g), wukv, gqcat, gkcat, gkr, invf, sgn)

    qspec = lambda t, w: pl.BlockSpec((1, t, w), lambda b, p, i: (b, i, p))
    kvspec = lambda w: pl.BlockSpec((1, S, w), lambda b, p, i: (b, 0, p))
    tq = 256
    const3 = lambda shape: pl.BlockSpec(shape, lambda b, p, i, *_: (0,) * len(shape))
    sb = pl.pallas_call(
        functools.partial(_sb_attn_kernel, tq=tq), grid=(B, SB_HEADS // 2, S // tq), name="sb_attn",
        in_specs=[qspec(tq, LANES), kvspec(LANES), kvspec(LANES)],
        out_specs=qspec(tq, LANES),
        out_shape=jax.ShapeDtypeStruct((B, S, SB_WIDTH), BF16),
    )(sbq.reshape(B, S, SB_WIDTH), sbk.reshape(B, S, SB_WIDTH), sbv.reshape(B, S, SB_WIDTH))
    tqm = 512
    gmax2 = lambda g: jnp.max(jnp.square(g))
    q_bound = jnp.sqrt(MLA_NOPE * gmax2(mla_qn_g) + MLA_ROPE * gmax2(mla_qr_g))
    k_bound = jnp.sqrt(MLA_NOPE * gmax2(mla_kn_g) + MLA_ROPE * gmax2(mla_kr_g))
    logit_bound = 1.02 * q_bound * k_bound * (qk ** -0.5) * LOG2E
    bounded = (logit_bound < MLA_SAFE_LOGIT).astype(jnp.int32).reshape(1)
    key_c = jnp.arange(tqm)[:, None] // CHUNK
    qry_c = (jnp.arange(2 * tqm)[None, :] % tqm) // CHUNK
    mla_diag = jnp.where(key_c <= qry_c, 0.0, -jnp.inf).astype(F32)
    mla = pl.pallas_call(
        functools.partial(_mla_attn_kernel, tq=tqm), name="mla_attn",
        grid_spec=pltpu.PrefetchScalarGridSpec(
            num_scalar_prefetch=1, grid=(B, N_PAIRS, S // tqm),
            in_specs=[pl.BlockSpec((1, tqm, PAIR_W), lambda b, p, i, *_: (b, i, p)),
                      pl.BlockSpec((1, S, PAIR_W), lambda b, p, i, *_: (b, 0, p)),
                      pl.BlockSpec((LANES, S), lambda b, p, i, *_: (p, b)),
                      const3((tqm, 2 * tqm))],
            out_specs=pl.BlockSpec((1, tqm, LANES), lambda b, p, i, *_: (b, i, p)),
            scratch_shapes=[pltpu.VMEM((1, 2 * tqm), F32), pltpu.VMEM((LANES, tqm), F32)]),
        out_shape=jax.ShapeDtypeStruct((B, S, MLA_WIDTH), BF16),
    )(bounded, mq.reshape(B, S, -1), mkk.reshape(B, S, -1), mvv, mla_diag)

    tp = 1024
    prow = lambda w: pl.BlockSpec((tp, w), lambda i: (i, 0))
    memspec = pl.BlockSpec((1, M, D), lambda i: ((i * tp) // S, 0, 0))
    W = D + LANES
    h2g, dest, cnt = pl.pallas_call(
        functools.partial(_post_attn_kernel, n_tokens=T), grid=(T // tp,), name="post_attn",
        in_specs=[prow(D), prow(SB_WIDTH), prow(MLA_WIDTH), full((1, SB_WIDTH)), full((1, MLA_WIDTH)),
                  full((D, D)), full((1, D)), full((D, D)), full((1, D // MEM_HEADS)), memspec, memspec,
                  full((D, D)), full((1, D)), full((D, 2 * LANES)), full((1, LANES))],
        out_specs=[prow(W), prow(1), pl.BlockSpec((1, 1, LANES), lambda i: (i, 0, 0))],
        out_shape=[jax.ShapeDtypeStruct((T, W), F32), jax.ShapeDtypeStruct((T, 1), jnp.int32),
                   jax.ShapeDtypeStruct((T // tp, 1, LANES), F32)],
        scratch_shapes=[pltpu.VMEM((1, LANES), F32)],
        compiler_params=pltpu.CompilerParams(dimension_semantics=("arbitrary",)),
    )(x.reshape(T, D), sb.reshape(T, SB_WIDTH), mla.reshape(T, MLA_WIDTH), _row(sb_out_g), _row(mla_out_g),
      w_o.astype(BF16), _row(norm_mem_g), w_mq.astype(BF16), _row(mem_qhead_g), mk, mv,
      w_mo.astype(BF16), _row(norm_moe_g), wr_cat, br)

    te = 256
    tg = 2048
    n_buckets = N_GROUPS * PAIRS_PER_GROUP
    n_steps = T // te + n_buckets
    counts = cnt[-1, 0, :n_buckets].astype(jnp.int32)
    tiles = (counts + te - 1) // te
    ends = jnp.cumsum(tiles)
    first_row = (ends - tiles) * te
    n_valid = ends[-1]
    slot = dest.reshape(T // LANES, LANES)
    slot_bucket = slot // T
    row0 = jnp.zeros_like(slot)
    for b in range(n_buckets):
        row0 = jnp.where(slot_bucket == b, first_row[b], row0)
    dest = (row0 + slot % T).reshape(T)
    step = jnp.minimum(jnp.arange(n_steps, dtype=jnp.int32), n_valid - 1)
    bucket = jnp.zeros_like(step)
    for b in range(n_buckets):
        bucket = bucket + (step >= ends[b]).astype(jnp.int32)
    pr = bucket % PAIRS_PER_GROUP
    pair_a = (pr >= 3).astype(jnp.int32) + (pr >= 5).astype(jnp.int32)
    pair_b = pr + 1 - 2 * (pr >= 3).astype(jnp.int32) - (pr >= 5).astype(jnp.int32)
    ea = bucket // PAIRS_PER_GROUP * EXPERTS_PER_GROUP + pair_a
    eb = bucket // PAIRS_PER_GROUP * EXPERTS_PER_GROUP + pair_b

    any_spec = pl.BlockSpec(memory_space=pl.ANY)
    h2s = pl.pallas_call(
        _scatter_rows_kernel, name="moe_dispatch",
        grid_spec=pltpu.PrefetchScalarGridSpec(
            num_scalar_prefetch=3, grid=(T // tg,),
            in_specs=[pl.BlockSpec((tg, W), lambda i, *_: (i, 0))],
            out_specs=any_spec,
            scratch_shapes=[pltpu.VMEM((te // 2, W), F32),
                            pltpu.SemaphoreType.DMA, pltpu.SemaphoreType.DMA]),
        out_shape=jax.ShapeDtypeStruct((n_steps * te, W), F32),
        compiler_params=pltpu.CompilerParams(dimension_semantics=("arbitrary",)),
    )(dest, first_row + counts, tiles * te - counts, h2g)

    wspec = lambda shape, which: pl.BlockSpec((1,) + shape, lambda i, blk, ea, eb, nv: ((ea, eb)[which][i], 0, 0))
    w_gate_b, w_up_b, w_down_b = w_gate.astype(BF16), w_up.astype(BF16), w_down.astype(BF16)
    ys = pl.pallas_call(
        _moe_pair_kernel, name="moe_experts",
        grid_spec=pltpu.PrefetchScalarGridSpec(
            num_scalar_prefetch=4, grid=(n_steps,),
            in_specs=[pl.BlockSpec((te, W), lambda i, blk, *_: (blk[i], 0)),
                      pl.BlockSpec((1, D), lambda i, *_: (0, 0)),
                      wspec((D, FF), 0), wspec((D, FF), 0), wspec((FF, D), 0),
                      wspec((D, FF), 1), wspec((D, FF), 1), wspec((FF, D), 1)],
            out_specs=pl.BlockSpec((te, D), lambda i, blk, *_: (blk[i], 0))),
        out_shape=jax.ShapeDtypeStruct((n_steps * te, D), F32),
        compiler_params=pltpu.CompilerParams(dimension_semantics=("arbitrary",)),
    )(step, ea, eb, n_valid.reshape(1), h2s, _row(norm_moe_g),
      w_gate_b, w_up_b, w_down_b, w_gate_b, w_up_b, w_down_b)

    out = pl.pallas_call(
        _gather_rows_kernel, name="moe_combine",
        grid_spec=pltpu.PrefetchScalarGridSpec(
            num_scalar_prefetch=2, grid=(T // tg,),
            in_specs=[any_spec],
            out_specs=pl.BlockSpec((tg // SUBLANES, SUBLANES, D), lambda i, *_: (i, 0, 0)),
            scratch_shapes=[pltpu.SemaphoreType.DMA]),
        out_shape=jax.ShapeDtypeStruct((T // SUBLANES, SUBLANES, D), F32),
        compiler_params=pltpu.CompilerParams(dimension_semantics=("arbitrary",)),
    )(dest // SUBLANES, dest % SUBLANES, ys.reshape(-1, SUBLANES, D))
    return out.reshape(B, S, D)
```

```python
import functools

import numpy as np
import jax
import jax.numpy as jnp
from jax import lax
from jax.experimental import pallas as pl
from jax.experimental.pallas import tpu as pltpu

F32 = jnp.float32
BF16 = jnp.bfloat16
EPS = 1e-6
LANES = 128
SUBLANES = 8

CHUNK = 64
SB_HEADS = 8
SB_HEAD_DIM = 64
SB_WIDTH = SB_HEADS * SB_HEAD_DIM
MLA_HEADS = 8
MLA_NOPE = 64
MLA_ROPE = 32
MLA_V = 64
MLA_WIDTH = MLA_HEADS * MLA_V
MLA_Q_RANK = 384
MLA_KV_RANK = 256
ROPE_THETA = 10000.0
MEM_HEADS = 4
N_GROUPS = 4
EXPERTS_PER_GROUP = 4
N_EXPERTS = N_GROUPS * EXPERTS_PER_GROUP
PAIRS_PER_GROUP = EXPERTS_PER_GROUP * (EXPERTS_PER_GROUP - 1) // 2
RANK_BLOCK = 256
POST_ATTN_CHUNKS = 2
N_PAIRS = MLA_HEADS // 2
PAIR_W = 2 * LANES
LOG2E = 1.4426950408889634
SB_EXP2_UNDERFLOW = 150.0
SB_DROPPED = -1e30
MLA_SAFE_LOGIT = 60.0

_TRANS_B = (((1,), (1,)), ((), ()))


def _rms(x, g):
    return x * lax.rsqrt(jnp.mean(x * x, axis=-1, keepdims=True) + EPS) * g


def _seg_rms(x, masks, seg_len, g):
    x2 = x * x
    r = jnp.zeros_like(x)
    for m in masks:
        s = jnp.sum(jnp.where(m, x2, 0.0), axis=-1, keepdims=True)
        r = jnp.where(m, lax.rsqrt(s * (1.0 / seg_len) + EPS), r)
    return x * r * g


def _mem_kv_kernel(mem_ref, gsrc_ref, w_ref, gk_ref, mk_ref, mv_ref):
    d = mem_ref.shape[-1]
    hd = d // MEM_HEADS
    mn = _rms(mem_ref[0], gsrc_ref[...]).astype(BF16)
    kv = jnp.dot(mn, w_ref[...], preferred_element_type=F32)
    for h in range(MEM_HEADS):
        kh = kv[:, h * hd:(h + 1) * hd]
        mk_ref[0, :, h * hd:(h + 1) * hd] = _rms(kh, gk_ref[...]).astype(BF16)
    mv_ref[0] = kv[:, d:].astype(BF16)


def _in_proj_kernel(x_ref, pos_ref, gmix_ref, wlat_ref, wsb_ref, gq_ref, wuq_ref, gkv_ref, wukv_ref,
                    gqcat_ref, gkcat_ref, gkr_ref, invf_ref, sgn_ref,
                    sbq_ref, sbk_ref, sbv_ref, mq_ref, mk_ref, mv_ref, cos_ref, sin_ref):
    ang = pos_ref[...].astype(F32) * invf_ref[...]
    cos_ref[...] = jnp.cos(ang)
    sin_ref[...] = jnp.sin(ang) * sgn_ref[...]

    hn = _rms(x_ref[...], gmix_ref[...]).astype(BF16)
    lat = jnp.dot(hn, wlat_ref[...], preferred_element_type=F32)
    cq = lat[:, 0:MLA_Q_RANK]
    ckv = lat[:, MLA_Q_RANK:MLA_Q_RANK + MLA_KV_RANK]
    kr = lat[:, MLA_Q_RANK + MLA_KV_RANK:]
    q = jnp.dot(_rms(cq, gq_ref[...]).astype(BF16), wuq_ref[...], preferred_element_type=F32)
    kv = jnp.dot(_rms(ckv, gkv_ref[...]).astype(BF16), wukv_ref[...], preferred_element_type=F32)

    sb = jnp.dot(hn, wsb_ref[...], preferred_element_type=F32)
    sbq_ref[...] = sb[:, 0:SB_WIDTH].astype(BF16)
    sbk_ref[...] = sb[:, SB_WIDTH:2 * SB_WIDTH].astype(BF16)
    sbv_ref[...] = sb[:, 2 * SB_WIDTH:3 * SB_WIDTH].astype(BF16)

    lane = lax.broadcasted_iota(jnp.int32, (1, LANES), 1)
    lr = lane & 63
    lo, hi = lane < 64, lane >= 64
    r0, r1 = lr < 16, (lr >= 16) & (lr < 32)

    def rope(b):
        return b * cos_ref[...] + pltpu.roll(b, 64, 1) * sin_ref[...]

    q_scale = (MLA_NOPE + MLA_ROPE) ** -0.5 * LOG2E
    krn = rope(_seg_rms(kr, (r0, r1), MLA_ROPE, gkr_ref[...]))
    for p in range(N_PAIRS):
        b0 = p * PAIR_W
        qa = _seg_rms(q[:, b0:b0 + LANES], (lo, hi), MLA_NOPE, gqcat_ref[:, b0:b0 + LANES])
        qb = rope(_seg_rms(q[:, b0 + LANES:b0 + PAIR_W], (r0, r1), MLA_ROPE,
                           gqcat_ref[:, b0 + LANES:b0 + PAIR_W]))
        mq_ref[:, b0:b0 + LANES] = (qa * q_scale).astype(BF16)
        mq_ref[:, b0 + LANES:b0 + PAIR_W] = (qb * q_scale).astype(BF16)
        ka = _seg_rms(kv[:, b0:b0 + LANES], (lo, hi), MLA_NOPE, gkcat_ref[:, b0:b0 + LANES])
        mk_ref[:, b0:b0 + LANES] = ka.astype(BF16)
        mk_ref[:, b0 + LANES:b0 + PAIR_W] = krn.astype(BF16)
    mv_ref[...] = kv[:, N_PAIRS * PAIR_W:].T.astype(BF16)


def _sb_attn_kernel(q_ref, k_ref, v_ref, o_ref, *, tq):
    i = pl.program_id(2)
    q = q_ref[0]
    lane = lax.broadcasted_iota(jnp.int32, (1, LANES), 1)
    lo = lane < 64
    zq = jnp.zeros_like(q)
    qcat = jnp.concatenate([jnp.where(lo, q, zq), jnp.where(lo, zq, q)], axis=0)
    row = lax.broadcasted_iota(jnp.int32, (tq, tq), 0)
    col = lax.broadcasted_iota(jnp.int32, (tq, tq), 1)
    later = (row > col).astype(BF16)
    past = jnp.concatenate([col < row, col < row], axis=0)

    def scores(j, mask=None):
        start = pl.multiple_of(j * tq, tq)
        z = lax.dot_general(qcat, k_ref[0, pl.ds(start, tq), :], _TRANS_B, preferred_element_type=F32)
        if mask is not None:
            z = mask(z)
        return z, jnp.maximum(z, 0.0) + jnp.log2(1.0 + jnp.exp2(-jnp.abs(z)))

    def weights(z, fail, after):
        between = after + jnp.dot(fail.astype(BF16), later, preferred_element_type=F32)
        return jnp.exp2((z - fail) - between).astype(BF16)

    def pv(ws, js):
        zv = jnp.zeros((tq, LANES), BF16)
        vs = [v_ref[0, pl.ds(pl.multiple_of(j * tq, tq), tq), :] for j in js]
        wcat = jnp.concatenate([w[:tq] for w in ws] + [w[tq:] for w in ws], axis=1)
        vcat = jnp.concatenate([jnp.where(lo, v, zv) for v in vs] + [jnp.where(lo, zv, v) for v in vs], axis=0)
        return jnp.dot(wcat, vcat, preferred_element_type=F32)

    prev = jnp.maximum(i - 1, 0)
    zb, fb = scores(i, lambda z: jnp.where(past, z, SB_DROPPED))
    za, fa = scores(prev, lambda z: jnp.where(i > 0, z, SB_DROPPED))
    tot_b = jnp.sum(fb, axis=1, keepdims=True)
    wb = weights(zb, fb, 0.0)
    wa = weights(za, fa, tot_b)
    carry0 = tot_b + jnp.sum(fa, axis=1, keepdims=True)
    acc0 = pv([wa, wb], [prev, i])

    def cond(st):
        j, m, _, _ = st
        return (j >= 0) & (m < SB_EXP2_UNDERFLOW)

    def body(st):
        j, _, carry, acc = st
        z, fail = scores(j)
        w = weights(z, fail, carry)
        carry = carry + jnp.sum(fail, axis=1, keepdims=True)
        return j - 1, jnp.min(carry), carry, acc + pv([w], [j])

    _, _, _, acc = lax.while_loop(cond, body, (i - 2, jnp.min(carry0), carry0, acc0))
    o_ref[0] = acc.astype(o_ref.dtype)


def _mla_attn_kernel(bounded_ref, q_ref, k_ref, v_ref, diag_ref, o_ref, l_ref, acc_ref, *, tq):
    i = pl.program_id(2)
    q = q_ref[0]
    l2 = lax.broadcasted_iota(jnp.int32, (1, PAIR_W), 1)
    lr = l2 & 63
    hm0 = (l2 < 64) | ((l2 >= LANES) & (lr < 16))
    hm1 = ((l2 >= 64) & (l2 < LANES)) | ((l2 >= LANES) & (lr >= 16) & (lr < 32))
    zq = jnp.zeros_like(q)
    qcat = jnp.concatenate([jnp.where(hm0, q, zq), jnp.where(hm1, q, zq)], axis=0)
    head0 = lax.broadcasted_iota(jnp.int32, (LANES, 1), 0) < 64

    def per_head(x):
        return jnp.where(head0, x[:, :tq], x[:, tq:])

    def logits(j, n, diag):
        start = pl.multiple_of(j * tq, tq)
        s = lax.dot_general(k_ref[0, pl.ds(start, n * tq), :], qcat, _TRANS_B, preferred_element_type=F32)
        if diag:
            tiles = [s[t * tq:(t + 1) * tq] for t in range(n)]
            s = jnp.concatenate(tiles[:-1] + [tiles[-1] + diag_ref[...]], axis=0)
        return s

    def pv(j, n, p):
        vt = v_ref[:, pl.ds(pl.multiple_of(j * tq, tq), n * tq)]
        pb = p.astype(BF16)
        zv = jnp.zeros_like(vt)
        return jnp.dot(jnp.concatenate([jnp.where(head0, vt, zv), jnp.where(head0, zv, vt)], axis=1),
                       jnp.concatenate([pb[:, :tq], pb[:, tq:]], axis=0),
                       preferred_element_type=F32)

    def finish(l, acc):
        o_ref[0] = (acc / per_head(l)).T.astype(o_ref.dtype)

    @pl.when(bounded_ref[0] != 0)
    def _():
        def step(j, n, diag):
            p = jnp.exp2(logits(j, n, diag))
            l_ref[...] += jnp.sum(p, axis=0, keepdims=True)
            acc_ref[...] += pv(j, n, p)

        l_ref[...] = jnp.zeros_like(l_ref)
        acc_ref[...] = jnp.zeros_like(acc_ref)

        @pl.loop(0, i // 4)
        def _(t):
            step(4 * t, 4, False)

        @pl.when(i % 4 >= 2)
        def _():
            step(i // 4 * 4, 2, False)

        @pl.when(i % 2 == 1)
        def _():
            step(i - 1, 2, True)

        @pl.when(i % 2 == 0)
        def _():
            step(i, 1, True)

        finish(l_ref[...], acc_ref[...])

    @pl.when(bounded_ref[0] == 0)
    def _():
        def step(j, st, diag):
            m, l, acc = st
            s = logits(j, 1, diag)
            m_new = jnp.maximum(m, jnp.max(s, axis=0, keepdims=True))
            alpha = jnp.exp2(m - m_new)
            p = jnp.exp2(s - m_new)
            l_new = alpha * l + jnp.sum(p, axis=0, keepdims=True)
            return m_new, l_new, per_head(alpha) * acc + pv(j, 1, p)

        st = step(i, (jnp.full((1, 2 * tq), -jnp.inf, F32), jnp.zeros((1, 2 * tq), F32),
                      jnp.zeros((LANES, tq), F32)), True)
        _, l, acc = lax.fori_loop(0, i, lambda j, st: step(j, st, False), st)
        finish(l, acc)


def _post_attn_kernel(x_ref, sb_ref, mla_ref, gsb_ref, gmla_ref, wo_ref, gmem_ref, wmq_ref, gqh_ref,
                      mk_ref, mv_ref, wmo_ref, gmoe_ref, wr_ref, br_ref,
                      h2g_ref, dest_ref, cnt_ref, carry_ref, *, n_tokens):
    d = x_ref.shape[-1]
    hd = d // MEM_HEADS
    tp = x_ref.shape[0]
    lane = lax.broadcasted_iota(jnp.int32, (1, LANES), 1).astype(F32)

    n_chunks = POST_ATTN_CHUNKS
    chunks = [pl.ds(c * (tp // n_chunks), tp // n_chunks) for c in range(n_chunks)]
    each = lambda f, *lists: [f(*args) for args in zip(*lists)]

    mixed = each(lambda r: jnp.concatenate(
        [_rms(sb_ref[r, :].astype(F32), gsb_ref[...]).astype(BF16),
         _rms(mla_ref[r, :].astype(F32), gmla_ref[...]).astype(BF16)], axis=1), chunks)
    h1 = each(lambda r, m: x_ref[r, :] + jnp.dot(m, wo_ref[...], preferred_element_type=F32), chunks, mixed)
    hq = each(lambda h: _rms(h, gmem_ref[...]).astype(BF16), h1)
    mq = each(lambda h: jnp.dot(h, wmq_ref[...], preferred_element_type=F32), hq)

    def mem_head(m, h):
        qh = (_rms(m[:, h * hd:(h + 1) * hd], gqh_ref[...]) * (hd ** -0.5)).astype(BF16)
        sc = lax.dot_general(qh, mk_ref[0, :, h * hd:(h + 1) * hd], _TRANS_B, preferred_element_type=F32)
        sc = sc - jnp.max(sc, axis=-1, keepdims=True)
        e = jnp.exp(sc)
        p = (e / jnp.sum(e, axis=-1, keepdims=True)).astype(BF16)
        return jnp.dot(p, mv_ref[0, :, h * hd:(h + 1) * hd], preferred_element_type=F32).astype(BF16)

    mos = [each(lambda m, h=h: mem_head(m, h), mq) for h in range(MEM_HEADS)]
    h2 = each(lambda h, *mo: h + jnp.dot(jnp.concatenate(mo, axis=1), wmo_ref[...],
                                         preferred_element_type=F32), h1, *mos)

    def route(rows, h2c):
        h2g_ref[rows, 0:d] = h2c
        t = _rms(h2c, gmoe_ref[...])
        t_hi = t.astype(BF16)
        t_lo = (t - t_hi.astype(F32)).astype(BF16)
        both = jnp.dot(t_hi, wr_ref[...], preferred_element_type=F32)
        logits = (both[:, :LANES] + both[:, LANES:]
                  + jnp.dot(t_lo, wr_ref[:, :LANES], preferred_element_type=F32)) + br_ref[...]

        big = float(LANES)
        neg = -jnp.inf
        lg = jnp.where(lane < N_GROUPS, logits, neg)
        gmax = jnp.max(lg, axis=-1, keepdims=True)
        g_idx = jnp.min(jnp.where(lg == gmax, lane, big), axis=-1, keepdims=True)
        g_w = 1.0 / jnp.sum(jnp.exp(lg - gmax), axis=-1, keepdims=True)
        e_lo = N_GROUPS + g_idx * EXPERTS_PER_GROUP
        in_grp = (lane >= e_lo) & (lane < e_lo + EXPERTS_PER_GROUP)
        le = jnp.where(in_grp, logits, neg)
        m1 = jnp.max(le, axis=-1, keepdims=True)
        i1 = jnp.min(jnp.where(le == m1, lane, big), axis=-1, keepdims=True)
        le2 = jnp.where(lane == i1, neg, le)
        m2 = jnp.max(le2, axis=-1, keepdims=True)
        i2 = jnp.min(jnp.where(le2 == m2, lane, big), axis=-1, keepdims=True)
        e2 = jnp.exp(m2 - m1)
        w1 = g_w / (1.0 + e2)
        w2 = g_w * e2 / (1.0 + e2)
        h2g_ref[rows, d:] = jnp.where(lane == i1, w1, 0.0) + jnp.where(lane == i2, w2, 0.0)

        ka = jnp.minimum(i1, i2) - e_lo
        kb = jnp.maximum(i1, i2) - e_lo
        pair = ka * (5.0 - ka) * 0.5 + kb - 1.0
        return g_idx * float(PAIRS_PER_GROUP) + pair

    bucket = jnp.concatenate(each(route, chunks, h2), axis=0)

    @pl.when(pl.program_id(0) == 0)
    def _():
        carry_ref[...] = jnp.zeros_like(carry_ref)

    blk = RANK_BLOCK
    earlier = (lax.broadcasted_iota(jnp.int32, (blk, blk), 1)
               < lax.broadcasted_iota(jnp.int32, (blk, blk), 0)).astype(BF16)
    carry = carry_ref[...]
    for r0 in range(0, tp, blk):
        bkt = bucket[r0:r0 + blk]
        oh = jnp.where(lane == bkt, 1.0, 0.0)
        prefix = jnp.dot(earlier, oh.astype(BF16), preferred_element_type=F32)
        rank = jnp.sum(jnp.where(lane == bkt, prefix + carry, 0.0), axis=-1, keepdims=True)
        dest_ref[r0:r0 + blk, :] = (bkt * float(n_tokens) + rank).astype(jnp.int32)
        carry = carry + jnp.sum(oh, axis=0, keepdims=True)
    carry_ref[...] = carry
    cnt_ref[0] = carry


def _scatter_rows_kernel(dest_ref, fill_start_ref, fill_n_ref, x_ref, o_hbm, zrow_ref, sem, zsem):
    i = pl.program_id(0)
    tg = x_ref.shape[0]
    base = i * tg

    @pl.loop(0, tg // SUBLANES)
    def _(r8):
        r0 = pl.multiple_of(r8 * SUBLANES, SUBLANES)
        for k in range(SUBLANES):
            pltpu.make_async_copy(x_ref.at[pl.ds(r0 + k, 1)],
                                  o_hbm.at[pl.ds(dest_ref[base + r0 + k], 1)], sem).start(priority=k % 2)

    pltpu.make_async_copy(x_ref, o_hbm.at[pl.ds(0, tg)], sem).wait()

    @pl.when(i == pl.num_programs(0) - 1)
    def _():
        zrow_ref[...] = jnp.zeros_like(zrow_ref)

        def fill(b, act):
            n = fill_n_ref[b]
            pos = fill_start_ref[b]
            head = jnp.minimum((-pos) & (SUBLANES - 1), n)
            zcopy = lambda at, size: pltpu.make_async_copy(zrow_ref.at[pl.ds(0, size)],
                                                           o_hbm.at[pl.ds(at, size)], zsem)
            for k in range(SUBLANES - 1):
                pl.when(k < head)(functools.partial(act, zcopy(pos + k, 1)))
            pos = pos + head
            n = n - head
            size = zrow_ref.shape[0]
            while size >= SUBLANES:
                pl.when((n & size) != 0)(functools.partial(act, zcopy(pl.multiple_of(pos, SUBLANES), size)))
                pos = pos + (n & size)
                size //= 2

        pl.loop(0, fill_n_ref.shape[0])(lambda b: fill(b, lambda c: c.start()))
        pl.loop(0, fill_n_ref.shape[0])(lambda b: fill(b, lambda c: c.wait()))


def _moe_pair_kernel(blk_ref, ea_ref, eb_ref, nvalid_ref, x_ref, gmoe_ref,
                     wga_ref, wua_ref, wda_ref, wgb_ref, wub_ref, wdb_ref, o_ref):
    i = pl.program_id(0)
    d = o_ref.shape[-1]

    @pl.when(i < nvalid_ref[0])
    def _():
        x = x_ref[...]
        h2 = x[:, 0:d]
        gates = x[:, d:]
        t = _rms(h2, gmoe_ref[...]).astype(BF16)
        lane = lax.broadcasted_iota(jnp.int32, (1, LANES), 1)

        def hidden(e, wg_ref, wu_ref):
            ge = jnp.sum(jnp.where(lane == N_GROUPS + e, gates, 0.0), axis=-1, keepdims=True)
            a = jnp.dot(t, wg_ref[0], preferred_element_type=F32)
            u = jnp.dot(t, wu_ref[0], preferred_element_type=F32)
            return (a * jax.nn.sigmoid(a) * u * ge).astype(BF16)

        ha = hidden(ea_ref[i], wga_ref, wua_ref)
        hb = hidden(eb_ref[i], wgb_ref, wub_ref)
        o_ref[...] = (h2 + jnp.dot(ha, wda_ref[0], preferred_element_type=F32)
                      + jnp.dot(hb, wdb_ref[0], preferred_element_type=F32))


def _gather_rows_kernel(tile_ref, sub_ref, y_hbm, o_ref, sem):
    n8 = o_ref.shape[0]
    base = pl.program_id(0) * n8 * SUBLANES

    @pl.loop(0, n8)
    def _(r8):
        r0 = base + r8 * SUBLANES
        for k in range(SUBLANES):
            pltpu.make_async_copy(y_hbm.at[tile_ref[r0 + k], pl.ds(sub_ref[r0 + k] & (SUBLANES - 1), 1)],
                                  o_ref.at[r8, pl.ds(k, 1)], sem).start(priority=k % 2)

    pltpu.make_async_copy(y_hbm.at[pl.ds(0, n8)], o_ref, sem).wait()


def _rope_block(x1, x2):
    z = jnp.zeros(x1.shape[:-1] + (32,), x1.dtype)
    return jnp.concatenate([x1, x1, z, x2, x2, z], axis=-1)


def _pair_rope_block(a, b):
    z = jnp.zeros(a.shape[:-1] + (32,), a.dtype)
    return jnp.concatenate([a[..., :16], b[..., :16], z, a[..., 16:], b[..., 16:], z], axis=-1)


def _row(v):
    return v.reshape(1, -1).astype(F32)


def kernel(x, mem, positions, norm_mix_g, w_in, mla_q_norm_g, w_uq, mla_kv_norm_g, w_ukv, mla_qn_g, mla_qr_g, mla_kn_g, mla_kr_g, sb_out_g, mla_out_g, w_o, norm_mem_g, mem_src_g, w_mq, w_mkv, mem_qhead_g, mem_khead_g, w_mo, norm_moe_g, w_group, b_group, w_router, b_router, w_gate, w_up, w_down):
    B, S, D = x.shape
    T = B * S
    M = mem.shape[1]
    FF = w_gate.shape[-1]
    half = MLA_ROPE // 2
    qk = MLA_NOPE + MLA_ROPE

    o = 3 * SB_WIDTH + MLA_Q_RANK + MLA_KV_RANK
    w_sb = jnp.concatenate([w_in[:, :SB_WIDTH] * (SB_HEAD_DIM ** -0.5 * LOG2E),
                            w_in[:, SB_WIDTH:3 * SB_WIDTH]], axis=1).astype(BF16)
    w_lat = jnp.concatenate([w_in[:, 3 * SB_WIDTH:o],
                             _rope_block(w_in[:, o:o + half], w_in[:, o + half:])], axis=1).astype(BF16)
    uq = w_uq.reshape(MLA_Q_RANK, MLA_HEADS, qk)
    ukv = w_ukv.reshape(MLA_KV_RANK, MLA_HEADS, MLA_NOPE + MLA_V)
    zk = jnp.zeros((MLA_KV_RANK, LANES), F32)
    uq_cols, uk_cols = [], []
    for p in range(N_PAIRS):
        a, b = 2 * p, 2 * p + 1
        uq_cols += [uq[:, a, :MLA_NOPE], uq[:, b, :MLA_NOPE], _pair_rope_block(uq[:, a, MLA_NOPE:], uq[:, b, MLA_NOPE:])]
        uk_cols += [ukv[:, a, :MLA_NOPE], ukv[:, b, :MLA_NOPE], zk]
    wuq = jnp.concatenate(uq_cols, axis=1).astype(BF16)
    wukv = jnp.concatenate(uk_cols + [ukv[:, h, MLA_NOPE:] for h in range(MLA_HEADS)], axis=1).astype(BF16)
    gq_pair = jnp.concatenate([mla_qn_g, mla_qn_g, _pair_rope_block(mla_qr_g, mla_qr_g)])
    gk_pair = jnp.concatenate([mla_kn_g, mla_kn_g, jnp.zeros((LANES,), F32)])
    gqcat = _row(jnp.tile(gq_pair, N_PAIRS))
    gkcat = _row(jnp.tile(gk_pair, N_PAIRS))
    gkr = _row(_rope_block(mla_kr_g[:half], mla_kr_g[half:]))
    inv_freq = ROPE_THETA ** (-(jnp.arange(half, dtype=F32) * 2.0 / MLA_ROPE))
    invf = _row(_rope_block(inv_freq, inv_freq))
    sgn = _row(jnp.concatenate([-jnp.ones((64,), F32), jnp.ones((64,), F32)]))
    wr = jnp.concatenate([w_group, w_router, jnp.zeros((D, LANES - N_GROUPS - N_EXPERTS), F32)], axis=1)
    wr_hi = wr.astype(BF16)
    wr_cat = jnp.concatenate([wr_hi, (wr - wr_hi.astype(F32)).astype(BF16)], axis=1)
    br = _row(jnp.concatenate([b_group, b_router, jnp.zeros((LANES - N_GROUPS - N_EXPERTS,), F32)]))

    full = lambda shape: pl.BlockSpec(shape, lambda *_: (0,) * len(shape))

    mk, mv = pl.pallas_call(
        _mem_kv_kernel, grid=(B,), name="mem_kv",
        in_specs=[pl.BlockSpec((1, M, D), lambda b: (b, 0, 0)), full((1, D)), full((D, 2 * D)),
                  full((1, D // MEM_HEADS))],
        out_specs=[pl.BlockSpec((1, M, D), lambda b: (b, 0, 0))] * 2,
        out_shape=[jax.ShapeDtypeStruct((B, M, D), BF16)] * 2,
    )(mem, _row(mem_src_g), w_mkv.astype(BF16), _row(mem_khead_g))

    tm = 512
    rows = lambda w: pl.BlockSpec((tm, w), lambda i: (i, 0))
    sbq, sbk, sbv, mq, mkk, mvv = pl.pallas_call(
        _in_proj_kernel, grid=(T // tm,), name="in_proj",
        in_specs=[rows(D), rows(1), full((1, D)), full(w_lat.shape), full(w_sb.shape),
                  full((1, MLA_Q_RANK)), full(wuq.shape),
                  full((1, MLA_KV_RANK)), full(wukv.shape), full(gqcat.shape), full(gkcat.shape),
                  full((1, LANES)), full((1, LANES)), full((1, LANES))],
        out_specs=[rows(SB_WIDTH)] * 3 + [rows(N_PAIRS * PAIR_W)] * 2
                  + [pl.BlockSpec((MLA_WIDTH, tm), lambda i: (0, i))],
        out_shape=[jax.ShapeDtypeStruct((T, SB_WIDTH), BF16)] * 3
                  + [jax.ShapeDtypeStruct((T, N_PAIRS * PAIR_W), BF16)] * 2
                  + [jax.ShapeDtypeStruct((MLA_WIDTH, T), BF16)],
        scratch_shapes=[pltpu.VMEM((tm, LANES), F32)] * 2,
    )(x.reshape(T, D), positions.reshape(T, 1), _row(norm_mix_g), w_lat, w_sb, _row(mla_q_norm_g), wuq,
      _row(mla_kv_norm_g), wukv, gqcat, gkcat, gkr, invf, sgn)

    qspec = lambda t, w: pl.BlockSpec((1, t, w), lambda b, p, i: (b, i, p))
    kvspec = lambda w: pl.BlockSpec((1, S, w), lambda b, p, i: (b, 0, p))
    tq = 256
    const3 = lambda shape: pl.BlockSpec(shape, lambda b, p, i, *_: (0,) * len(shape))
    sb = pl.pallas_call(
        functools.partial(_sb_attn_kernel, tq=tq), grid=(B, SB_HEADS // 2, S // tq), name="sb_attn",
        in_specs=[qspec(tq, LANES), kvspec(LANES), kvspec(LANES)],
        out_specs=qspec(tq, LANES),
        out_shape=jax.ShapeDtypeStruct((B, S, SB_WIDTH), BF16),
    )(sbq.reshape(B, S, SB_WIDTH), sbk.reshape(B, S, SB_WIDTH), sbv.reshape(B, S, SB_WIDTH))
    tqm = 512
    gmax2 = lambda g: jnp.max(jnp.square(g))
    q_bound = jnp.sqrt(MLA_NOPE * gmax2(mla_qn_g) + MLA_ROPE * gmax2(mla_qr_g))
    k_bound = jnp.sqrt(MLA_NOPE * gmax2(mla_kn_g) + MLA_ROPE * gmax2(mla_kr_g))
    logit_bound = 1.02 * q_bound * k_bound * (qk ** -0.5) * LOG2E
    bounded = (logit_bound < MLA_SAFE_LOGIT).astype(jnp.int32).reshape(1)
    key_c = jnp.arange(tqm)[:, None] // CHUNK
    qry_c = (jnp.arange(2 * tqm)[None, :] % tqm) // CHUNK
    mla_diag = jnp.where(key_c <= qry_c, 0.0, -jnp.inf).astype(F32)
    mla = pl.pallas_call(
        functools.partial(_mla_attn_kernel, tq=tqm), name="mla_attn",
        grid_spec=pltpu.PrefetchScalarGridSpec(
            num_scalar_prefetch=1, grid=(B, N_PAIRS, S // tqm),
            in_specs=[pl.BlockSpec((1, tqm, PAIR_W), lambda b, p, i, *_: (b, i, p)),
                      pl.BlockSpec((1, S, PAIR_W), lambda b, p, i, *_: (b, 0, p)),
                      pl.BlockSpec((LANES, S), lambda b, p, i, *_: (p, b)),
                      const3((tqm, 2 * tqm))],
            out_specs=pl.BlockSpec((1, tqm, LANES), lambda b, p, i, *_: (b, i, p)),
            scratch_shapes=[pltpu.VMEM((1, 2 * tqm), F32), pltpu.VMEM((LANES, tqm), F32)]),
        out_shape=jax.ShapeDtypeStruct((B, S, MLA_WIDTH), BF16),
    )(bounded, mq.reshape(B, S, -1), mkk.reshape(B, S, -1), mvv, mla_diag)

    tp = 1024
    prow = lambda w: pl.BlockSpec((tp, w), lambda i: (i, 0))
    memspec = pl.BlockSpec((1, M, D), lambda i: ((i * tp) // S, 0, 0))
    W = D + LANES
    h2g, dest, cnt = pl.pallas_call(
        functools.partial(_post_attn_kernel, n_tokens=T), grid=(T // tp,), name="post_attn",
        in_specs=[prow(D), prow(SB_WIDTH), prow(MLA_WIDTH), full((1, SB_WIDTH)), full((1, MLA_WIDTH)),
                  full((D, D)), full((1, D)), full((D, D)), full((1, D // MEM_HEADS)), memspec, memspec,
                  full((D, D)), full((1, D)), full((D, 2 * LANES)), full((1, LANES))],
        out_specs=[prow(W), prow(1), pl.BlockSpec((1, 1, LANES), lambda i: (i, 0, 0))],
        out_shape=[jax.ShapeDtypeStruct((T, W), F32), jax.ShapeDtypeStruct((T, 1), jnp.int32),
                   jax.ShapeDtypeStruct((T // tp, 1, LANES), F32)],
        scratch_shapes=[pltpu.VMEM((1, LANES), F32)],
        compiler_params=pltpu.CompilerParams(dimension_semantics=("arbitrary",)),
    )(x.reshape(T, D), sb.reshape(T, SB_WIDTH), mla.reshape(T, MLA_WIDTH), _row(sb_out_g), _row(mla_out_g),
      w_o.astype(BF16), _row(norm_mem_g), w_mq.astype(BF16), _row(mem_qhead_g), mk, mv,
      w_mo.astype(BF16), _row(norm_moe_g), wr_cat, br)

    te = 256
    tg = 2048
    n_buckets = N_GROUPS * PAIRS_PER_GROUP
    n_steps = T // te + n_buckets
    counts = cnt[-1, 0, :n_buckets].astype(jnp.int32)
    tiles = (counts + te - 1) // te
    ends = jnp.cumsum(tiles)
    first_row = (ends - tiles) * te
    n_valid = ends[-1]
    slot = dest.reshape(T // LANES, LANES)
    slot_bucket = slot // T
    row0 = jnp.zeros_like(slot)
    for b in range(n_buckets):
        row0 = jnp.where(slot_bucket == b, first_row[b], row0)
    dest = (row0 + slot % T).reshape(T)
    step = jnp.minimum(jnp.arange(n_steps, dtype=jnp.int32), n_valid - 1)
    bucket = jnp.zeros_like(step)
    for b in range(n_buckets):
        bucket = bucket + (step >= ends[b]).astype(jnp.int32)
    pr = bucket % PAIRS_PER_GROUP
    pair_a = (pr >= 3).astype(jnp.int32) + (pr >= 5).astype(jnp.int32)
    pair_b = pr + 1 - 2 * (pr >= 3).astype(jnp.int32) - (pr >= 5).astype(jnp.int32)
    ea = bucket // PAIRS_PER_GROUP * EXPERTS_PER_GROUP + pair_a
    eb = bucket // PAIRS_PER_GROUP * EXPERTS_PER_GROUP + pair_b

    any_spec = pl.BlockSpec(memory_space=pl.ANY)
    h2s = pl.pallas_call(
        _scatter_rows_kernel, name="moe_dispatch",
        grid_spec=pltpu.PrefetchScalarGridSpec(
            num_scalar_prefetch=3, grid=(T // tg,),
            in_specs=[pl.BlockSpec((tg, W), lambda i, *_: (i, 0))],
            out_specs=any_spec,
            scratch_shapes=[pltpu.VMEM((te // 2, W), F32),
                            pltpu.SemaphoreType.DMA, pltpu.SemaphoreType.DMA]),
        out_shape=jax.ShapeDtypeStruct((n_steps * te, W), F32),
        compiler_params=pltpu.CompilerParams(dimension_semantics=("arbitrary",)),
    )(dest, first_row + counts, tiles * te - counts, h2g)

    wspec = lambda shape, which: pl.BlockSpec((1,) + shape, lambda i, blk, ea, eb, nv: ((ea, eb)[which][i], 0, 0))
    w_gate_b, w_up_b, w_down_b = w_gate.astype(BF16), w_up.astype(BF16), w_down.astype(BF16)
    ys = pl.pallas_call(
        _moe_pair_kernel, name="moe_experts",
        grid_spec=pltpu.PrefetchScalarGridSpec(
            num_scalar_prefetch=4, grid=(n_steps,),
            in_specs=[pl.BlockSpec((te, W), lambda i, blk, *_: (blk[i], 0)),
                      pl.BlockSpec((1, D), lambda i, *_: (0, 0)),
                      wspec((D, FF), 0), wspec((D, FF), 0), wspec((FF, D), 0),
                      wspec((D, FF), 1), wspec((D, FF), 1), wspec((FF, D), 1)],
            out_specs=pl.BlockSpec((te, D), lambda i, blk, *_: (blk[i], 0))),
        out_shape=jax.ShapeDtypeStruct((n_steps * te, D), F32),
        compiler_params=pltpu.CompilerParams(dimension_semantics=("arbitrary",)),
    )(step, ea, eb, n_valid.reshape(1), h2s, _row(norm_moe_g),
      w_gate_b, w_up_b, w_down_b, w_gate_b, w_up_b, w_down_b)

    out = pl.pallas_call(
        _gather_rows_kernel, name="moe_combine",
        grid_spec=pltpu.PrefetchScalarGridSpec(
            num_scalar_prefetch=2, grid=(T // tg,),
            in_specs=[any_spec],
            out_specs=pl.BlockSpec((tg // SUBLANES, SUBLANES, D), lambda i, *_: (i, 0, 0)),
            scratch_shapes=[pltpu.SemaphoreType.DMA]),
        out_shape=jax.ShapeDtypeStruct((T // SUBLANES, SUBLANES, D), F32),
        compiler_params=pltpu.CompilerParams(dimension_semantics=("arbitrary",)),
    )(dest // SUBLANES, dest % SUBLANES, ys.reshape(-1, SUBLANES, D))
    return out.reshape(B, S, D)
```

```python
import functools

import numpy as np
import jax
import jax.numpy as jnp
from jax import lax
from jax.experimental import pallas as pl
from jax.experimental.pallas import tpu as pltpu

F32 = jnp.float32
BF16 = jnp.bfloat16
EPS = 1e-6
LANES = 128
SUBLANES = 8

CHUNK = 64
SB_HEADS = 8
SB_HEAD_DIM = 64
SB_WIDTH = SB_HEADS * SB_HEAD_DIM
MLA_HEADS = 8
MLA_NOPE = 64
MLA_ROPE = 32
MLA_V = 64
MLA_WIDTH = MLA_HEADS * MLA_V
MLA_Q_RANK = 384
MLA_KV_RANK = 256
ROPE_THETA = 10000.0
MEM_HEADS = 4
N_GROUPS = 4
EXPERTS_PER_GROUP = 4
N_EXPERTS = N_GROUPS * EXPERTS_PER_GROUP
PAIRS_PER_GROUP = EXPERTS_PER_GROUP * (EXPERTS_PER_GROUP - 1) // 2
RANK_BLOCK = 256
POST_ATTN_CHUNKS = 2
N_PAIRS = MLA_HEADS // 2
PAIR_W = 2 * LANES
LOG2E = 1.4426950408889634
SB_EXP2_UNDERFLOW = 150.0
SB_DROPPED = -1e30
MLA_SAFE_LOGIT = 60.0

_TRANS_B = (((1,), (1,)), ((), ()))


def _rms(x, g):
    return x * lax.rsqrt(jnp.mean(x * x, axis=-1, keepdims=True) + EPS) * g


def _seg_rms(x, masks, seg_len, g):
    x2 = x * x
    r = jnp.zeros_like(x)
    for m in masks:
        s = jnp.sum(jnp.where(m, x2, 0.0), axis=-1, keepdims=True)
        r = jnp.where(m, lax.rsqrt(s * (1.0 / seg_len) + EPS), r)
    return x * r * g


def _mem_kv_kernel(mem_ref, gsrc_ref, w_ref, gk_ref, mk_ref, mv_ref):
    d = mem_ref.shape[-1]
    hd = d // MEM_HEADS
    mn = _rms(mem_ref[0], gsrc_ref[...]).astype(BF16)
    kv = jnp.dot(mn, w_ref[...], preferred_element_type=F32)
    for h in range(MEM_HEADS):
        kh = kv[:, h * hd:(h + 1) * hd]
        mk_ref[0, :, h * hd:(h + 1) * hd] = _rms(kh, gk_ref[...]).astype(BF16)
    mv_ref[0] = kv[:, d:].astype(BF16)


def _in_proj_kernel(x_ref, pos_ref, gmix_ref, wlat_ref, wsb_ref, gq_ref, wuq_ref, gkv_ref, wukv_ref,
                    gqcat_ref, gkcat_ref, gkr_ref, invf_ref, sgn_ref,
                    sbq_ref, sbk_ref, sbv_ref, mq_ref, mk_ref, mv_ref, cos_ref, sin_ref):
    ang = pos_ref[...].astype(F32) * invf_ref[...]
    cos_ref[...] = jnp.cos(ang)
    sin_ref[...] = jnp.sin(ang) * sgn_ref[...]

    hn = _rms(x_ref[...], gmix_ref[...]).astype(BF16)
    lat = jnp.dot(hn, wlat_ref[...], preferred_element_type=F32)
    cq = lat[:, 0:MLA_Q_RANK]
    ckv = lat[:, MLA_Q_RANK:MLA_Q_RANK + MLA_KV_RANK]
    kr = lat[:, MLA_Q_RANK + MLA_KV_RANK:]
    q = jnp.dot(_rms(cq, gq_ref[...]).astype(BF16), wuq_ref[...], preferred_element_type=F32)
    kv = jnp.dot(_rms(ckv, gkv_ref[...]).astype(BF16), wukv_ref[...], preferred_element_type=F32)

    sb = jnp.dot(hn, wsb_ref[...], preferred_element_type=F32)
    sbq_ref[...] = sb[:, 0:SB_WIDTH].astype(BF16)
    sbk_ref[...] = sb[:, SB_WIDTH:2 * SB_WIDTH].astype(BF16)
    sbv_ref[...] = sb[:, 2 * SB_WIDTH:3 * SB_WIDTH].astype(BF16)

    lane = lax.broadcasted_iota(jnp.int32, (1, LANES), 1)
    lr = lane & 63
    lo, hi = lane < 64, lane >= 64
    r0, r1 = lr < 16, (lr >= 16) & (lr < 32)

    def rope(b):
        return b * cos_ref[...] + pltpu.roll(b, 64, 1) * sin_ref[...]

    q_scale = (MLA_NOPE + MLA_ROPE) ** -0.5 * LOG2E
    krn = rope(_seg_rms(kr, (r0, r1), MLA_ROPE, gkr_ref[...]))
    for p in range(N_PAIRS):
        b0 = p * PAIR_W
        qa = _seg_rms(q[:, b0:b0 + LANES], (lo, hi), MLA_NOPE, gqcat_ref[:, b0:b0 + LANES])
        qb = rope(_seg_rms(q[:, b0 + LANES:b0 + PAIR_W], (r0, r1), MLA_ROPE,
                           gqcat_ref[:, b0 + LANES:b0 + PAIR_W]))
        mq_ref[:, b0:b0 + LANES] = (qa * q_scale).astype(BF16)
        mq_ref[:, b0 + LANES:b0 + PAIR_W] = (qb * q_scale).astype(BF16)
        ka = _seg_rms(kv[:, b0:b0 + LANES], (lo, hi), MLA_NOPE, gkcat_ref[:, b0:b0 + LANES])
        mk_ref[:, b0:b0 + LANES] = ka.astype(BF16)
        mk_ref[:, b0 + LANES:b0 + PAIR_W] = krn.astype(BF16)
    mv_ref[...] = kv[:, N_PAIRS * PAIR_W:].T.astype(BF16)


def _sb_attn_kernel(q_ref, k_ref, v_ref, o_ref, *, tq):
    i = pl.program_id(2)
    q = q_ref[0]
    lane = lax.broadcasted_iota(jnp.int32, (1, LANES), 1)
    lo = lane < 64
    zq = jnp.zeros_like(q)
    qcat = jnp.concatenate([jnp.where(lo, q, zq), jnp.where(lo, zq, q)], axis=0)
    row = lax.broadcasted_iota(jnp.int32, (tq, tq), 0)
    col = lax.broadcasted_iota(jnp.int32, (tq, tq), 1)
    later = (row > col).astype(BF16)
    past = jnp.concatenate([col < row, col < row], axis=0)

    def scores(j, mask=None):
        start = pl.multiple_of(j * tq, tq)
        z = lax.dot_general(qcat, k_ref[0, pl.ds(start, tq), :], _TRANS_B, preferred_element_type=F32)
        if mask is not None:
            z = mask(z)
        return z, jnp.maximum(z, 0.0) + jnp.log2(1.0 + jnp.exp2(-jnp.abs(z)))

    def weights(z, fail, after):
        between = after + jnp.dot(fail.astype(BF16), later, preferred_element_type=F32)
        return jnp.exp2((z - fail) - between).astype(BF16)

    def pv(ws, js):
        zv = jnp.zeros((tq, LANES), BF16)
        vs = [v_ref[0, pl.ds(pl.multiple_of(j * tq, tq), tq), :] for j in js]
        wcat = jnp.concatenate([w[:tq] for w in ws] + [w[tq:] for w in ws], axis=1)
        vcat = jnp.concatenate([jnp.where(lo, v, zv) for v in vs] + [jnp.where(lo, zv, v) for v in vs], axis=0)
        return jnp.dot(wcat, vcat, preferred_element_type=F32)

    prev = jnp.maximum(i - 1, 0)
    zb, fb = scores(i, lambda z: jnp.where(past, z, SB_DROPPED))
    za, fa = scores(prev, lambda z: jnp.where(i > 0, z, SB_DROPPED))
    tot_b = jnp.sum(fb, axis=1, keepdims=True)
    wb = weights(zb, fb, 0.0)
    wa = weights(za, fa, tot_b)
    carry0 = tot_b + jnp.sum(fa, axis=1, keepdims=True)
    acc0 = pv([wa, wb], [prev, i])

    def cond(st):
        j, m, _, _ = st
        return (j >= 0) & (m < SB_EXP2_UNDERFLOW)

    def body(st):
        j, _, carry, acc = st
        z, fail = scores(j)
        w = weights(z, fail, carry)
        carry = carry + jnp.sum(fail, axis=1, keepdims=True)
        return j - 1, jnp.min(carry), carry, acc + pv([w], [j])

    _, _, _, acc = lax.while_loop(cond, body, (i - 2, jnp.min(carry0), carry0, acc0))
    o_ref[0] = acc.astype(o_ref.dtype)


def _mla_attn_kernel(bounded_ref, q_ref, k_ref, v_ref, diag_ref, o_ref, l_ref, acc_ref, *, tq):
    i = pl.program_id(2)
    q = q_ref[0]
    l2 = lax.broadcasted_iota(jnp.int32, (1, PAIR_W), 1)
    lr = l2 & 63
    hm0 = (l2 < 64) | ((l2 >= LANES) & (lr < 16))
    hm1 = ((l2 >= 64) & (l2 < LANES)) | ((l2 >= LANES) & (lr >= 16) & (lr < 32))
    zq = jnp.zeros_like(q)
    qcat = jnp.concatenate([jnp.where(hm0, q, zq), jnp.where(hm1, q, zq)], axis=0)
    head0 = lax.broadcasted_iota(jnp.int32, (LANES, 1), 0) < 64

    def per_head(x):
        return jnp.where(head0, x[:, :tq], x[:, tq:])

    def logits(j, n, diag):
        start = pl.multiple_of(j * tq, tq)
        s = lax.dot_general(k_ref[0, pl.ds(start, n * tq), :], qcat, _TRANS_B, preferred_element_type=F32)
        if diag:
            tiles = [s[t * tq:(t + 1) * tq] for t in range(n)]
            s = jnp.concatenate(tiles[:-1] + [tiles[-1] + diag_ref[...]], axis=0)
        return s

    def pv(j, n, p):
        vt = v_ref[:, pl.ds(pl.multiple_of(j * tq, tq), n * tq)]
        pb = p.astype(BF16)
        zv = jnp.zeros_like(vt)
        return jnp.dot(jnp.concatenate([jnp.where(head0, vt, zv), jnp.where(head0, zv, vt)], axis=1),
                       jnp.concatenate([pb[:, :tq], pb[:, tq:]], axis=0),
                       preferred_element_type=F32)

    def finish(l, acc):
        o_ref[0] = (acc / per_head(l)).T.astype(o_ref.dtype)

    @pl.when(bounded_ref[0] != 0)
    def _():
        def step(j, n, diag):
            p = jnp.exp2(logits(j, n, diag))
            l_ref[...] += jnp.sum(p, axis=0, keepdims=True)
            acc_ref[...] += pv(j, n, p)

        l_ref[...] = jnp.zeros_like(l_ref)
        acc_ref[...] = jnp.zeros_like(acc_ref)

        @pl.loop(0, i // 4)
        def _(t):
            step(4 * t, 4, False)

        @pl.when(i % 4 >= 2)
        def _():
            step(i // 4 * 4, 2, False)

        @pl.when(i % 2 == 1)
        def _():
            step(i - 1, 2, True)

        @pl.when(i % 2 == 0)
        def _():
            step(i, 1, True)

        finish(l_ref[...], acc_ref[...])

    @pl.when(bounded_ref[0] == 0)
    def _():
        def step(j, st, diag):
            m, l, acc = st
            s = logits(j, 1, diag)
            m_new = jnp.maximum(m, jnp.max(s, axis=0, keepdims=True))
            alpha = jnp.exp2(m - m_new)
            p = jnp.exp2(s - m_new)
            l_new = alpha * l + jnp.sum(p, axis=0, keepdims=True)
            return m_new, l_new, per_head(alpha) * acc + pv(j, 1, p)

        st = step(i, (jnp.full((1, 2 * tq), -jnp.inf, F32), jnp.zeros((1, 2 * tq), F32),
                      jnp.zeros((LANES, tq), F32)), True)
        _, l, acc = lax.fori_loop(0, i, lambda j, st: step(j, st, False), st)
        finish(l, acc)


def _post_attn_kernel(x_ref, sb_ref, mla_ref, gsb_ref, gmla_ref, wo_ref, gmem_ref, wmq_ref, gqh_ref,
                      mk_ref, mv_ref, wmo_ref, gmoe_ref, wr_ref, br_ref,
                      h2g_ref, dest_ref, cnt_ref, carry_ref, *, n_tokens):
    d = x_ref.shape[-1]
    hd = d // MEM_HEADS
    tp = x_ref.shape[0]
    lane = lax.broadcasted_iota(jnp.int32, (1, LANES), 1).astype(F32)

    n_chunks = POST_ATTN_CHUNKS
    chunks = [pl.ds(c * (tp // n_chunks), tp // n_chunks) for c in range(n_chunks)]
    each = lambda f, *lists: [f(*args) for args in zip(*lists)]

    mixed = each(lambda r: jnp.concatenate(
        [_rms(sb_ref[r, :].astype(F32), gsb_ref[...]).astype(BF16),
         _rms(mla_ref[r, :].astype(F32), gmla_ref[...]).astype(BF16)], axis=1), chunks)
    h1 = each(lambda r, m: x_ref[r, :] + jnp.dot(m, wo_ref[...], preferred_element_type=F32), chunks, mixed)
    hq = each(lambda h: _rms(h, gmem_ref[...]).astype(BF16), h1)
    mq = each(lambda h: jnp.dot(h, wmq_ref[...], preferred_element_type=F32), hq)

    def mem_head(m, h):
        qh = (_rms(m[:, h * hd:(h + 1) * hd], gqh_ref[...]) * (hd ** -0.5)).astype(BF16)
        sc = lax.dot_general(qh, mk_ref[0, :, h * hd:(h + 1) * hd], _TRANS_B, preferred_element_type=F32)
        sc = sc - jnp.max(sc, axis=-1, keepdims=True)
        e = jnp.exp(sc)
        p = (e / jnp.sum(e, axis=-1, keepdims=True)).astype(BF16)
        return jnp.dot(p, mv_ref[0, :, h * hd:(h + 1) * hd], preferred_element_type=F32).astype(BF16)

    mos = [each(lambda m, h=h: mem_head(m, h), mq) for h in range(MEM_HEADS)]
    h2 = each(lambda h, *mo: h + jnp.dot(jnp.concatenate(mo, axis=1), wmo_ref[...],
                                         preferred_element_type=F32), h1, *mos)

    def route(rows, h2c):
        h2g_ref[rows, 0:d] = h2c
        t = _rms(h2c, gmoe_ref[...])
        t_hi = t.astype(BF16)
        t_lo = (t - t_hi.astype(F32)).astype(BF16)
        both = jnp.dot(t_hi, wr_ref[...], preferred_element_type=F32)
        logits = (both[:, :LANES] + both[:, LANES:]
                  + jnp.dot(t_lo, wr_ref[:, :LANES], preferred_element_type=F32)) + br_ref[...]

        big = float(LANES)
        neg = -jnp.inf
        lg = jnp.where(lane < N_GROUPS, logits, neg)
        gmax = jnp.max(lg, axis=-1, keepdims=True)
        g_idx = jnp.min(jnp.where(lg == gmax, lane, big), axis=-1, keepdims=True)
        g_w = 1.0 / jnp.sum(jnp.exp(lg - gmax), axis=-1, keepdims=True)
        e_lo = N_GROUPS + g_idx * EXPERTS_PER_GROUP
        in_grp = (lane >= e_lo) & (lane < e_lo + EXPERTS_PER_GROUP)
        le = jnp.where(in_grp, logits, neg)
        m1 = jnp.max(le, axis=-1, keepdims=True)
        i1 = jnp.min(jnp.where(le == m1, lane, big), axis=-1, keepdims=True)
        le2 = jnp.where(lane == i1, neg, le)
        m2 = jnp.max(le2, axis=-1, keepdims=True)
        i2 = jnp.min(jnp.where(le2 == m2, lane, big), axis=-1, keepdims=True)
        e2 = jnp.exp(m2 - m1)
        w1 = g_w / (1.0 + e2)
        w2 = g_w * e2 / (1.0 + e2)
        h2g_ref[rows, d:] = jnp.where(lane == i1, w1, 0.0) + jnp.where(lane == i2, w2, 0.0)

        ka = jnp.minimum(i1, i2) - e_lo
        kb = jnp.maximum(i1, i2) - e_lo
        pair = ka * (5.0 - ka) * 0.5 + kb - 1.0
        return g_idx * float(PAIRS_PER_GROUP) + pair

    bucket = jnp.concatenate(each(route, chunks, h2), axis=0)

    @pl.when(pl.program_id(0) == 0)
    def _():
        carry_ref[...] = jnp.zeros_like(carry_ref)

    blk = RANK_BLOCK
    earlier = (lax.broadcasted_iota(jnp.int32, (blk, blk), 1)
               < lax.broadcasted_iota(jnp.int32, (blk, blk), 0)).astype(BF16)
    carry = carry_ref[...]
    for r0 in range(0, tp, blk):
        bkt = bucket[r0:r0 + blk]
        oh = jnp.where(lane == bkt, 1.0, 0.0)
        prefix = jnp.dot(earlier, oh.astype(BF16), preferred_element_type=F32)
        rank = jnp.sum(jnp.where(lane == bkt, prefix + carry, 0.0), axis=-1, keepdims=True)
        dest_ref[r0:r0 + blk, :] = (bkt * float(n_tokens) + rank).astype(jnp.int32)
        carry = carry + jnp.sum(oh, axis=0, keepdims=True)
    carry_ref[...] = carry
    cnt_ref[0] = carry


def _scatter_rows_kernel(dest_ref, fill_start_ref, fill_n_ref, x_ref, o_hbm, zrow_ref, sem, zsem):
    i = pl.program_id(0)
    tg = x_ref.shape[0]
    base = i * tg

    @pl.loop(0, tg // SUBLANES)
    def _(r8):
        r0 = pl.multiple_of(r8 * SUBLANES, SUBLANES)
        for k in range(SUBLANES):
            pltpu.make_async_copy(x_ref.at[pl.ds(r0 + k, 1)],
                                  o_hbm.at[pl.ds(dest_ref[base + r0 + k], 1)], sem).start()

    pltpu.make_async_copy(x_ref, o_hbm.at[pl.ds(0, tg)], sem).wait()

    @pl.when(i == pl.num_programs(0) - 1)
    def _():
        zrow_ref[...] = jnp.zeros_like(zrow_ref)

        def fill(b, act):
            n = fill_n_ref[b]
            pos = fill_start_ref[b]
            head = jnp.minimum((-pos) & (SUBLANES - 1), n)
            zcopy = lambda at, size: pltpu.make_async_copy(zrow_ref.at[pl.ds(0, size)],
                                                           o_hbm.at[pl.ds(at, size)], zsem)
            for k in range(SUBLANES - 1):
                pl.when(k < head)(functools.partial(act, zcopy(pos + k, 1)))
            pos = pos + head
            n = n - head
            size = zrow_ref.shape[0]
            while size >= SUBLANES:
                pl.when((n & size) != 0)(functools.partial(act, zcopy(pl.multiple_of(pos, SUBLANES), size)))
                pos = pos + (n & size)
                size //= 2

        pl.loop(0, fill_n_ref.shape[0])(lambda b: fill(b, lambda c: c.start()))
        pl.loop(0, fill_n_ref.shape[0])(lambda b: fill(b, lambda c: c.wait()))


def _moe_pair_kernel(blk_ref, ea_ref, eb_ref, nvalid_ref, x_ref, gmoe_ref,
                     wga_ref, wua_ref, wda_ref, wgb_ref, wub_ref, wdb_ref, o_ref):
    i = pl.program_id(0)
    d = o_ref.shape[-1]

    @pl.when(i < nvalid_ref[0])
    def _():
        x = x_ref[...]
        h2 = x[:, 0:d]
        gates = x[:, d:]
        t = _rms(h2, gmoe_ref[...]).astype(BF16)
        lane = lax.broadcasted_iota(jnp.int32, (1, LANES), 1)

        def hidden(e, wg_ref, wu_ref):
            ge = jnp.sum(jnp.where(lane == N_GROUPS + e, gates, 0.0), axis=-1, keepdims=True)
            a = jnp.dot(t, wg_ref[0], preferred_element_type=F32)
            u = jnp.dot(t, wu_ref[0], preferred_element_type=F32)
            return (a * jax.nn.sigmoid(a) * u * ge).astype(BF16)

        ha = hidden(ea_ref[i], wga_ref, wua_ref)
        hb = hidden(eb_ref[i], wgb_ref, wub_ref)
        o_ref[...] = (h2 + jnp.dot(ha, wda_ref[0], preferred_element_type=F32)
                      + jnp.dot(hb, wdb_ref[0], preferred_element_type=F32))


def _gather_rows_kernel(tile_ref, sub_ref, y_hbm, o_ref, sem):
    n8 = o_ref.shape[0]
    base = pl.program_id(0) * n8 * SUBLANES

    @pl.loop(0, n8)
    def _(r8):
        r0 = base + r8 * SUBLANES
        for k in range(SUBLANES):
            pltpu.make_async_copy(y_hbm.at[tile_ref[r0 + k], pl.ds(sub_ref[r0 + k] & (SUBLANES - 1), 1)],
                                  o_ref.at[r8, pl.ds(k, 1)], sem).start()

    pltpu.make_async_copy(y_hbm.at[pl.ds(0, n8)], o_ref, sem).wait()


def _rope_block(x1, x2):
    z = jnp.zeros(x1.shape[:-1] + (32,), x1.dtype)
    return jnp.concatenate([x1, x1, z, x2, x2, z], axis=-1)


def _pair_rope_block(a, b):
    z = jnp.zeros(a.shape[:-1] + (32,), a.dtype)
    return jnp.concatenate([a[..., :16], b[..., :16], z, a[..., 16:], b[..., 16:], z], axis=-1)


def _row(v):
    return v.reshape(1, -1).astype(F32)


def kernel(x, mem, positions, norm_mix_g, w_in, mla_q_norm_g, w_uq, mla_kv_norm_g, w_ukv, mla_qn_g, mla_qr_g, mla_kn_g, mla_kr_g, sb_out_g, mla_out_g, w_o, norm_mem_g, mem_src_g, w_mq, w_mkv, mem_qhead_g, mem_khead_g, w_mo, norm_moe_g, w_group, b_group, w_router, b_router, w_gate, w_up, w_down):
    B, S, D = x.shape
    T = B * S
    M = mem.shape[1]
    FF = w_gate.shape[-1]
    half = MLA_ROPE // 2
    qk = MLA_NOPE + MLA_ROPE

    o = 3 * SB_WIDTH + MLA_Q_RANK + MLA_KV_RANK
    w_sb = jnp.concatenate([w_in[:, :SB_WIDTH] * (SB_HEAD_DIM ** -0.5 * LOG2E),
                            w_in[:, SB_WIDTH:3 * SB_WIDTH]], axis=1).astype(BF16)
    w_lat = jnp.concatenate([w_in[:, 3 * SB_WIDTH:o],
                             _rope_block(w_in[:, o:o + half], w_in[:, o + half:])], axis=1).astype(BF16)
    uq = w_uq.reshape(MLA_Q_RANK, MLA_HEADS, qk)
    ukv = w_ukv.reshape(MLA_KV_RANK, MLA_HEADS, MLA_NOPE + MLA_V)
    zk = jnp.zeros((MLA_KV_RANK, LANES), F32)
    uq_cols, uk_cols = [], []
    for p in range(N_PAIRS):
        a, b = 2 * p, 2 * p + 1
        uq_cols += [uq[:, a, :MLA_NOPE], uq[:, b, :MLA_NOPE], _pair_rope_block(uq[:, a, MLA_NOPE:], uq[:, b, MLA_NOPE:])]
        uk_cols += [ukv[:, a, :MLA_NOPE], ukv[:, b, :MLA_NOPE], zk]
    wuq = jnp.concatenate(uq_cols, axis=1).astype(BF16)
    wukv = jnp.concatenate(uk_cols + [ukv[:, h, MLA_NOPE:] for h in range(MLA_HEADS)], axis=1).astype(BF16)
    gq_pair = jnp.concatenate([mla_qn_g, mla_qn_g, _pair_rope_block(mla_qr_g, mla_qr_g)])
    gk_pair = jnp.concatenate([mla_kn_g, mla_kn_g, jnp.zeros((LANES,), F32)])
    gqcat = _row(jnp.tile(gq_pair, N_PAIRS))
    gkcat = _row(jnp.tile(gk_pair, N_PAIRS))
    gkr = _row(_rope_block(mla_kr_g[:half], mla_kr_g[half:]))
    inv_freq = ROPE_THETA ** (-(jnp.arange(half, dtype=F32) * 2.0 / MLA_ROPE))
    invf = _row(_rope_block(inv_freq, inv_freq))
    sgn = _row(jnp.concatenate([-jnp.ones((64,), F32), jnp.ones((64,), F32)]))
    wr = jnp.concatenate([w_group, w_router, jnp.zeros((D, LANES - N_GROUPS - N_EXPERTS), F32)], axis=1)
    wr_hi = wr.astype(BF16)
    wr_cat = jnp.concatenate([wr_hi, (wr - wr_hi.astype(F32)).astype(BF16)], axis=1)
    br = _row(jnp.concatenate([b_group, b_router, jnp.zeros((LANES - N_GROUPS - N_EXPERTS,), F32)]))

    full = lambda shape: pl.BlockSpec(shape, lambda *_: (0,) * len(shape))

    mk, mv = pl.pallas_call(
        _mem_kv_kernel, grid=(B,), name="mem_kv",
        in_specs=[pl.BlockSpec((1, M, D), lambda b: (b, 0, 0)), full((1, D)), full((D, 2 * D)),
                  full((1, D // MEM_HEADS))],
        out_specs=[pl.BlockSpec((1, M, D), lambda b: (b, 0, 0))] * 2,
        out_shape=[jax.ShapeDtypeStruct((B, M, D), BF16)] * 2,
    )(mem, _row(mem_src_g), w_mkv.astype(BF16), _row(mem_khead_g))

    tm = 1024
    rows = lambda w: pl.BlockSpec((tm, w), lambda i: (i, 0))
    sbq, sbk, sbv, mq, mkk, mvv = pl.pallas_call(
        _in_proj_kernel, grid=(T // tm,), name="in_proj",
        in_specs=[rows(D), rows(1), full((1, D)), full(w_lat.shape), full(w_sb.shape),
                  full((1, MLA_Q_RANK)), full(wuq.shape),
                  full((1, MLA_KV_RANK)), full(wukv.shape), full(gqcat.shape), full(gkcat.shape),
                  full((1, LANES)), full((1, LANES)), full((1, LANES))],
        out_specs=[rows(SB_WIDTH)] * 3 + [rows(N_PAIRS * PAIR_W)] * 2
                  + [pl.BlockSpec((MLA_WIDTH, tm), lambda i: (0, i))],
        out_shape=[jax.ShapeDtypeStruct((T, SB_WIDTH), BF16)] * 3
                  + [jax.ShapeDtypeStruct((T, N_PAIRS * PAIR_W), BF16)] * 2
                  + [jax.ShapeDtypeStruct((MLA_WIDTH, T), BF16)],
        scratch_shapes=[pltpu.VMEM((tm, LANES), F32)] * 2,
    )(x.reshape(T, D), positions.reshape(T, 1), _row(norm_mix_g), w_lat, w_sb, _row(mla_q_norm_g), wuq,
      _row(mla_kv_norm_g), wukv, gqcat, gkcat, gkr, invf, sgn)

    qspec = lambda t, w: pl.BlockSpec((1, t, w), lambda b, p, i: (b, i, p))
    kvspec = lambda w: pl.BlockSpec((1, S, w), lambda b, p, i: (b, 0, p))
    tq = 256
    const3 = lambda shape: pl.BlockSpec(shape, lambda b, p, i, *_: (0,) * len(shape))
    sb = pl.pallas_call(
        functools.partial(_sb_attn_kernel, tq=tq), grid=(B, SB_HEADS // 2, S // tq), name="sb_attn",
        in_specs=[qspec(tq, LANES), kvspec(LANES), kvspec(LANES)],
        out_specs=qspec(tq, LANES),
        out_shape=jax.ShapeDtypeStruct((B, S, SB_WIDTH), BF16),
    )(sbq.reshape(B, S, SB_WIDTH), sbk.reshape(B, S, SB_WIDTH), sbv.reshape(B, S, SB_WIDTH))
    tqm = 512
    gmax2 = lambda g: jnp.max(jnp.square(g))
    q_bound = jnp.sqrt(MLA_NOPE * gmax2(mla_qn_g) + MLA_ROPE * gmax2(mla_qr_g))
    k_bound = jnp.sqrt(MLA_NOPE * gmax2(mla_kn_g) + MLA_ROPE * gmax2(mla_kr_g))
    logit_bound = 1.02 * q_bound * k_bound * (qk ** -0.5) * LOG2E
    bounded = (logit_bound < MLA_SAFE_LOGIT).astype(jnp.int32).reshape(1)
    key_c = jnp.arange(tqm)[:, None] // CHUNK
    qry_c = (jnp.arange(2 * tqm)[None, :] % tqm) // CHUNK
    mla_diag = jnp.where(key_c <= qry_c, 0.0, -jnp.inf).astype(F32)
    mla = pl.pallas_call(
        functools.partial(_mla_attn_kernel, tq=tqm), name="mla_attn",
        grid_spec=pltpu.PrefetchScalarGridSpec(
            num_scalar_prefetch=1, grid=(B, N_PAIRS, S // tqm),
            in_specs=[pl.BlockSpec((1, tqm, PAIR_W), lambda b, p, i, *_: (b, i, p)),
                      pl.BlockSpec((1, S, PAIR_W), lambda b, p, i, *_: (b, 0, p)),
                      pl.BlockSpec((LANES, S), lambda b, p, i, *_: (p, b)),
                      const3((tqm, 2 * tqm))],
            out_specs=pl.BlockSpec((1, tqm, LANES), lambda b, p, i, *_: (b, i, p)),
            scratch_shapes=[pltpu.VMEM((1, 2 * tqm), F32), pltpu.VMEM((LANES, tqm), F32)]),
        out_shape=jax.ShapeDtypeStruct((B, S, MLA_WIDTH), BF16),
    )(bounded, mq.reshape(B, S, -1), mkk.reshape(B, S, -1), mvv, mla_diag)

    tp = 1024
    prow = lambda w: pl.BlockSpec((tp, w), lambda i: (i, 0))
    memspec = pl.BlockSpec((1, M, D), lambda i: ((i * tp) // S, 0, 0))
    W = D + LANES
    h2g, dest, cnt = pl.pallas_call(
        functools.partial(_post_attn_kernel, n_tokens=T), grid=(T // tp,), name="post_attn",
        in_specs=[prow(D), prow(SB_WIDTH), prow(MLA_WIDTH), full((1, SB_WIDTH)), full((1, MLA_WIDTH)),
                  full((D, D)), full((1, D)), full((D, D)), full((1, D // MEM_HEADS)), memspec, memspec,
                  full((D, D)), full((1, D)), full((D, 2 * LANES)), full((1, LANES))],
        out_specs=[prow(W), prow(1), pl.BlockSpec((1, 1, LANES), lambda i: (i, 0, 0))],
        out_shape=[jax.ShapeDtypeStruct((T, W), F32), jax.ShapeDtypeStruct((T, 1), jnp.int32),
                   jax.ShapeDtypeStruct((T // tp, 1, LANES), F32)],
        scratch_shapes=[pltpu.VMEM((1, LANES), F32)],
        compiler_params=pltpu.CompilerParams(dimension_semantics=("arbitrary",)),
    )(x.reshape(T, D), sb.reshape(T, SB_WIDTH), mla.reshape(T, MLA_WIDTH), _row(sb_out_g), _row(mla_out_g),
      w_o.astype(BF16), _row(norm_mem_g), w_mq.astype(BF16), _row(mem_qhead_g), mk, mv,
      w_mo.astype(BF16), _row(norm_moe_g), wr_cat, br)

    te = 256
    tg = 2048
    n_buckets = N_GROUPS * PAIRS_PER_GROUP
    n_steps = T // te + n_buckets
    counts = cnt[-1, 0, :n_buckets].astype(jnp.int32)
    tiles = (counts + te - 1) // te
    ends = jnp.cumsum(tiles)
    first_row = (ends - tiles) * te
    n_valid = ends[-1]
    slot = dest.reshape(T // LANES, LANES)
    slot_bucket = slot // T
    row0 = jnp.zeros_like(slot)
    for b in range(n_buckets):
        row0 = jnp.where(slot_bucket == b, first_row[b], row0)
    dest = (row0 + slot % T).reshape(T)
    step = jnp.minimum(jnp.arange(n_steps, dtype=jnp.int32), n_valid - 1)
    bucket = jnp.zeros_like(step)
    for b in range(n_buckets):
        bucket = bucket + (step >= ends[b]).astype(jnp.int32)
    pr = bucket % PAIRS_PER_GROUP
    pair_a = (pr >= 3).astype(jnp.int32) + (pr >= 5).astype(jnp.int32)
    pair_b = pr + 1 - 2 * (pr >= 3).astype(jnp.int32) - (pr >= 5).astype(jnp.int32)
    ea = bucket // PAIRS_PER_GROUP * EXPERTS_PER_GROUP + pair_a
    eb = bucket // PAIRS_PER_GROUP * EXPERTS_PER_GROUP + pair_b

    any_spec = pl.BlockSpec(memory_space=pl.ANY)
    h2s = pl.pallas_call(
        _scatter_rows_kernel, name="moe_dispatch",
        grid_spec=pltpu.PrefetchScalarGridSpec(
            num_scalar_prefetch=3, grid=(T // tg,),
            in_specs=[pl.BlockSpec((tg, W), lambda i, *_: (i, 0))],
            out_specs=any_spec,
            scratch_shapes=[pltpu.VMEM((te // 2, W), F32),
                            pltpu.SemaphoreType.DMA, pltpu.SemaphoreType.DMA]),
        out_shape=jax.ShapeDtypeStruct((n_steps * te, W), F32),
        compiler_params=pltpu.CompilerParams(dimension_semantics=("arbitrary",)),
    )(dest, first_row + counts, tiles * te - counts, h2g)

    wspec = lambda shape, which: pl.BlockSpec((1,) + shape, lambda i, blk, ea, eb, nv: ((ea, eb)[which][i], 0, 0))
    w_gate_b, w_up_b, w_down_b = w_gate.astype(BF16), w_up.astype(BF16), w_down.astype(BF16)
    ys = pl.pallas_call(
        _moe_pair_kernel, name="moe_experts",
        grid_spec=pltpu.PrefetchScalarGridSpec(
            num_scalar_prefetch=4, grid=(n_steps,),
            in_specs=[pl.BlockSpec((te, W), lambda i, blk, *_: (blk[i], 0)),
                      pl.BlockSpec((1, D), lambda i, *_: (0, 0)),
                      wspec((D, FF), 0), wspec((D, FF), 0), wspec((FF, D), 0),
                      wspec((D, FF), 1), wspec((D, FF), 1), wspec((FF, D), 1)],
            out_specs=pl.BlockSpec((te, D), lambda i, blk, *_: (blk[i], 0))),
        out_shape=jax.ShapeDtypeStruct((n_steps * te, D), F32),
        compiler_params=pltpu.CompilerParams(dimension_semantics=("arbitrary",)),
    )(step, ea, eb, n_valid.reshape(1), h2s, _row(norm_moe_g),
      w_gate_b, w_up_b, w_down_b, w_gate_b, w_up_b, w_down_b)

    out = pl.pallas_call(
        _gather_rows_kernel, name="moe_combine",
        grid_spec=pltpu.PrefetchScalarGridSpec(
            num_scalar_prefetch=2, grid=(T // tg,),
            in_specs=[any_spec],
            out_specs=pl.BlockSpec((tg // SUBLANES, SUBLANES, D), lambda i, *_: (i, 0, 0)),
            scratch_shapes=[pltpu.SemaphoreType.DMA]),
        out_shape=jax.ShapeDtypeStruct((T // SUBLANES, SUBLANES, D), F32),
        compiler_params=pltpu.CompilerParams(dimension_semantics=("arbitrary",)),
    )(dest // SUBLANES, dest % SUBLANES, ys.reshape(-1, SUBLANES, D))
    return out.reshape(B, S, D)
```

```python
import functools

import jax
import jax.numpy as jnp
from jax import lax
from jax.experimental import pallas as pl
from jax.experimental.pallas import tpu as pltpu

F32 = jnp.float32
BF16 = jnp.bfloat16
EPS = 1e-6
LANES = 128
SUBLANES = 8

CHUNK = 64
SB_HEADS = 8
SB_HEAD_DIM = 64
SB_WIDTH = SB_HEADS * SB_HEAD_DIM
MLA_HEADS = 8
MLA_NOPE = 64
MLA_ROPE = 32
MLA_V = 64
MLA_WIDTH = MLA_HEADS * MLA_V
MLA_Q_RANK = 384
MLA_KV_RANK = 256
ROPE_THETA = 10000.0
MEM_HEADS = 4
N_GROUPS = 4
EXPERTS_PER_GROUP = 4
N_EXPERTS = N_GROUPS * EXPERTS_PER_GROUP
PAIRS_PER_GROUP = EXPERTS_PER_GROUP * (EXPERTS_PER_GROUP - 1) // 2
RANK_BLOCK = 256
TILE_IN_PROJ = 1024
TILE_SB_Q = 256
TILE_MLA_Q = 512
TILE_POST_ATTN = 1024
TILE_EXPERT = 256
TILE_MOVE = 2048
POST_ATTN_CHUNKS = 2
N_PAIRS = MLA_HEADS // 2
PAIR_W = 2 * LANES
LOG2E = 1.4426950408889634
SB_EXP2_UNDERFLOW = 150.0
SB_DROPPED = -1e30
MLA_SAFE_LOGIT = 60.0

_TRANS_B = (((1,), (1,)), ((), ()))


def _rms(x, g):
    return x * lax.rsqrt(jnp.mean(x * x, axis=-1, keepdims=True) + EPS) * g


def _seg_rms(x, masks, seg_len, g):
    x2 = x * x
    r = jnp.zeros_like(x)
    for m in masks:
        s = jnp.sum(jnp.where(m, x2, 0.0), axis=-1, keepdims=True)
        r = jnp.where(m, lax.rsqrt(s * (1.0 / seg_len) + EPS), r)
    return x * r * g


def _mem_kv_kernel(mem_ref, gsrc_ref, w_ref, gk_ref, mk_ref, mv_ref):
    d = mem_ref.shape[-1]
    hd = d // MEM_HEADS
    mn = _rms(mem_ref[0], gsrc_ref[...]).astype(BF16)
    kv = jnp.dot(mn, w_ref[...], preferred_element_type=F32)
    for h in range(MEM_HEADS):
        kh = kv[:, h * hd:(h + 1) * hd]
        mk_ref[0, :, h * hd:(h + 1) * hd] = _rms(kh, gk_ref[...]).astype(BF16)
    mv_ref[0] = kv[:, d:].astype(BF16)


def _in_proj_kernel(x_ref, pos_ref, gmix_ref, wlat_ref, wsb_ref, gq_ref, wuq_ref, gkv_ref, wukv_ref,
                    gqcat_ref, gkcat_ref, gkr_ref, invf_ref, sgn_ref,
                    sbq_ref, sbk_ref, sbv_ref, mq_ref, mk_ref, mv_ref, cos_ref, sin_ref):
    ang = pos_ref[...].astype(F32) * invf_ref[...]
    cos_ref[...] = jnp.cos(ang)
    sin_ref[...] = jnp.sin(ang) * sgn_ref[...]

    hn = _rms(x_ref[...], gmix_ref[...]).astype(BF16)
    lat = jnp.dot(hn, wlat_ref[...], preferred_element_type=F32)
    cq = lat[:, 0:MLA_Q_RANK]
    ckv = lat[:, MLA_Q_RANK:MLA_Q_RANK + MLA_KV_RANK]
    kr = lat[:, MLA_Q_RANK + MLA_KV_RANK:]
    q = jnp.dot(_rms(cq, gq_ref[...]).astype(BF16), wuq_ref[...], preferred_element_type=F32)
    kv = jnp.dot(_rms(ckv, gkv_ref[...]).astype(BF16), wukv_ref[...], preferred_element_type=F32)

    sb = jnp.dot(hn, wsb_ref[...], preferred_element_type=F32)
    sbq_ref[...] = sb[:, 0:SB_WIDTH].astype(BF16)
    sbk_ref[...] = sb[:, SB_WIDTH:2 * SB_WIDTH].astype(BF16)
    sbv_ref[...] = sb[:, 2 * SB_WIDTH:3 * SB_WIDTH].astype(BF16)

    lane = lax.broadcasted_iota(jnp.int32, (1, LANES), 1)
    lr = lane & 63
    lo, hi = lane < 64, lane >= 64
    r0, r1 = lr < 16, (lr >= 16) & (lr < 32)

    def rope(b):
        return b * cos_ref[...] + pltpu.roll(b, 64, 1) * sin_ref[...]

    q_scale = (MLA_NOPE + MLA_ROPE) ** -0.5 * LOG2E
    krn = rope(_seg_rms(kr, (r0, r1), MLA_ROPE, gkr_ref[...]))
    for p in range(N_PAIRS):
        b0 = p * PAIR_W
        qa = _seg_rms(q[:, b0:b0 + LANES], (lo, hi), MLA_NOPE, gqcat_ref[:, b0:b0 + LANES])
        qb = rope(_seg_rms(q[:, b0 + LANES:b0 + PAIR_W], (r0, r1), MLA_ROPE,
                           gqcat_ref[:, b0 + LANES:b0 + PAIR_W]))
        mq_ref[:, b0:b0 + LANES] = (qa * q_scale).astype(BF16)
        mq_ref[:, b0 + LANES:b0 + PAIR_W] = (qb * q_scale).astype(BF16)
        ka = _seg_rms(kv[:, b0:b0 + LANES], (lo, hi), MLA_NOPE, gkcat_ref[:, b0:b0 + LANES])
        mk_ref[:, b0:b0 + LANES] = ka.astype(BF16)
        mk_ref[:, b0 + LANES:b0 + PAIR_W] = krn.astype(BF16)
    mv_ref[...] = kv[:, N_PAIRS * PAIR_W:].T.astype(BF16)


def _sb_attn_kernel(q_ref, k_ref, v_ref, o_ref, *, tq):
    i = pl.program_id(2)
    q = q_ref[0]
    lane = lax.broadcasted_iota(jnp.int32, (1, LANES), 1)
    lo = lane < 64
    zq = jnp.zeros_like(q)
    qcat = jnp.concatenate([jnp.where(lo, q, zq), jnp.where(lo, zq, q)], axis=0)
    row = lax.broadcasted_iota(jnp.int32, (tq, tq), 0)
    col = lax.broadcasted_iota(jnp.int32, (tq, tq), 1)
    later = (row > col).astype(BF16)
    past = jnp.concatenate([col < row, col < row], axis=0)

    def scores(j, mask=None):
        start = pl.multiple_of(j * tq, tq)
        z = lax.dot_general(qcat, k_ref[0, pl.ds(start, tq), :], _TRANS_B, preferred_element_type=F32)
        if mask is not None:
            z = mask(z)
        return z, jnp.maximum(z, 0.0) + jnp.log2(1.0 + jnp.exp2(-jnp.abs(z)))

    def weights(z, fail, after):
        between = after + jnp.dot(fail.astype(BF16), later, preferred_element_type=F32)
        return jnp.exp2((z - fail) - between).astype(BF16)

    def pv(ws, js):
        zv = jnp.zeros((tq, LANES), BF16)
        vs = [v_ref[0, pl.ds(pl.multiple_of(j * tq, tq), tq), :] for j in js]
        wcat = jnp.concatenate([w[:tq] for w in ws] + [w[tq:] for w in ws], axis=1)
        vcat = jnp.concatenate([jnp.where(lo, v, zv) for v in vs] + [jnp.where(lo, zv, v) for v in vs], axis=0)
        return jnp.dot(wcat, vcat, preferred_element_type=F32)

    prev = jnp.maximum(i - 1, 0)
    zb, fb = scores(i, lambda z: jnp.where(past, z, SB_DROPPED))
    za, fa = scores(prev, lambda z: jnp.where(i > 0, z, SB_DROPPED))
    tot_b = jnp.sum(fb, axis=1, keepdims=True)
    wb = weights(zb, fb, 0.0)
    wa = weights(za, fa, tot_b)
    carry0 = tot_b + jnp.sum(fa, axis=1, keepdims=True)
    acc0 = pv([wa, wb], [prev, i])

    def cond(st):
        j, m, _, _ = st
        return (j >= 0) & (m < SB_EXP2_UNDERFLOW)

    def body(st):
        j, _, carry, acc = st
        z, fail = scores(j)
        w = weights(z, fail, carry)
        carry = carry + jnp.sum(fail, axis=1, keepdims=True)
        return j - 1, jnp.min(carry), carry, acc + pv([w], [j])

    _, _, _, acc = lax.while_loop(cond, body, (i - 2, jnp.min(carry0), carry0, acc0))
    o_ref[0] = acc.astype(o_ref.dtype)


def _mla_attn_kernel(bounded_ref, q_ref, k_ref, v_ref, diag_ref, o_ref, l_ref, acc_ref, *, tq):
    i = pl.program_id(2)
    q = q_ref[0]
    l2 = lax.broadcasted_iota(jnp.int32, (1, PAIR_W), 1)
    lr = l2 & 63
    hm0 = (l2 < 64) | ((l2 >= LANES) & (lr < 16))
    hm1 = ((l2 >= 64) & (l2 < LANES)) | ((l2 >= LANES) & (lr >= 16) & (lr < 32))
    zq = jnp.zeros_like(q)
    qcat = jnp.concatenate([jnp.where(hm0, q, zq), jnp.where(hm1, q, zq)], axis=0)
    head0 = lax.broadcasted_iota(jnp.int32, (LANES, 1), 0) < 64

    def per_head(x):
        return jnp.where(head0, x[:, :tq], x[:, tq:])

    def logits(j, n, diag):
        start = pl.multiple_of(j * tq, tq)
        s = lax.dot_general(k_ref[0, pl.ds(start, n * tq), :], qcat, _TRANS_B, preferred_element_type=F32)
        if diag:
            tiles = [s[t * tq:(t + 1) * tq] for t in range(n)]
            s = jnp.concatenate(tiles[:-1] + [tiles[-1] + diag_ref[...]], axis=0)
        return s

    def pv(j, n, p):
        vt = v_ref[:, pl.ds(pl.multiple_of(j * tq, tq), n * tq)]
        pb = p.astype(BF16)
        zv = jnp.zeros_like(vt)
        return jnp.dot(jnp.concatenate([jnp.where(head0, vt, zv), jnp.where(head0, zv, vt)], axis=1),
                       jnp.concatenate([pb[:, :tq], pb[:, tq:]], axis=0),
                       preferred_element_type=F32)

    def finish(l, acc):
        o_ref[0] = (acc / per_head(l)).T.astype(o_ref.dtype)

    @pl.when(bounded_ref[0] != 0)
    def _():
        def step(j, n, diag):
            p = jnp.exp2(logits(j, n, diag))
            l_ref[...] += jnp.sum(p, axis=0, keepdims=True)
            acc_ref[...] += pv(j, n, p)

        l_ref[...] = jnp.zeros_like(l_ref)
        acc_ref[...] = jnp.zeros_like(acc_ref)

        @pl.loop(0, i // 4)
        def _(t):
            step(4 * t, 4, False)

        @pl.when(i % 4 >= 2)
        def _():
            step(i // 4 * 4, 2, False)

        @pl.when(i % 2 == 1)
        def _():
            step(i - 1, 2, True)

        @pl.when(i % 2 == 0)
        def _():
            step(i, 1, True)

        finish(l_ref[...], acc_ref[...])

    @pl.when(bounded_ref[0] == 0)
    def _():
        def step(j, st, diag):
            m, l, acc = st
            s = logits(j, 1, diag)
            m_new = jnp.maximum(m, jnp.max(s, axis=0, keepdims=True))
            alpha = jnp.exp2(m - m_new)
            p = jnp.exp2(s - m_new)
            l_new = alpha * l + jnp.sum(p, axis=0, keepdims=True)
            return m_new, l_new, per_head(alpha) * acc + pv(j, 1, p)

        st = step(i, (jnp.full((1, 2 * tq), -jnp.inf, F32), jnp.zeros((1, 2 * tq), F32),
                      jnp.zeros((LANES, tq), F32)), True)
        _, l, acc = lax.fori_loop(0, i, lambda j, st: step(j, st, False), st)
        finish(l, acc)


def _post_attn_kernel(x_ref, sb_ref, mla_ref, gsb_ref, gmla_ref, wo_ref, gmem_ref, wmq_ref, gqh_ref,
                      mk_ref, mv_ref, wmo_ref, gmoe_ref, wr_ref, br_ref,
                      h2g_ref, dest_ref, cnt_ref, carry_ref, *, n_tokens):
    d = x_ref.shape[-1]
    hd = d // MEM_HEADS
    tp = x_ref.shape[0]
    lane = lax.broadcasted_iota(jnp.int32, (1, LANES), 1).astype(F32)

    n_chunks = POST_ATTN_CHUNKS
    chunks = [pl.ds(c * (tp // n_chunks), tp // n_chunks) for c in range(n_chunks)]
    each = lambda f, *lists: [f(*args) for args in zip(*lists)]

    mixed = each(lambda r: jnp.concatenate(
        [_rms(sb_ref[r, :].astype(F32), gsb_ref[...]).astype(BF16),
         _rms(mla_ref[r, :].astype(F32), gmla_ref[...]).astype(BF16)], axis=1), chunks)
    h1 = each(lambda r, m: x_ref[r, :] + jnp.dot(m, wo_ref[...], preferred_element_type=F32), chunks, mixed)
    hq = each(lambda h: _rms(h, gmem_ref[...]).astype(BF16), h1)
    mq = each(lambda h: jnp.dot(h, wmq_ref[...], preferred_element_type=F32), hq)

    def mem_head(m, h):
        qh = (_rms(m[:, h * hd:(h + 1) * hd], gqh_ref[...]) * (hd ** -0.5)).astype(BF16)
        sc = lax.dot_general(qh, mk_ref[0, :, h * hd:(h + 1) * hd], _TRANS_B, preferred_element_type=F32)
        sc = sc - jnp.max(sc, axis=-1, keepdims=True)
        e = jnp.exp(sc)
        p = (e / jnp.sum(e, axis=-1, keepdims=True)).astype(BF16)
        return jnp.dot(p, mv_ref[0, :, h * hd:(h + 1) * hd], preferred_element_type=F32).astype(BF16)

    mos = [each(lambda m, h=h: mem_head(m, h), mq) for h in range(MEM_HEADS)]
    h2 = each(lambda h, *mo: h + jnp.dot(jnp.concatenate(mo, axis=1), wmo_ref[...],
                                         preferred_element_type=F32), h1, *mos)

    def route(rows, h2c):
        h2g_ref[rows, 0:d] = h2c
        t = _rms(h2c, gmoe_ref[...])
        t_hi = t.astype(BF16)
        t_lo = (t - t_hi.astype(F32)).astype(BF16)
        both = jnp.dot(t_hi, wr_ref[...], preferred_element_type=F32)
        logits = (both[:, :LANES] + both[:, LANES:]
                  + jnp.dot(t_lo, wr_ref[:, :LANES], preferred_element_type=F32)) + br_ref[...]

        big = float(LANES)
        neg = -jnp.inf
        lg = jnp.where(lane < N_GROUPS, logits, neg)
        gmax = jnp.max(lg, axis=-1, keepdims=True)
        g_idx = jnp.min(jnp.where(lg == gmax, lane, big), axis=-1, keepdims=True)
        g_w = 1.0 / jnp.sum(jnp.exp(lg - gmax), axis=-1, keepdims=True)
        e_lo = N_GROUPS + g_idx * EXPERTS_PER_GROUP
        in_grp = (lane >= e_lo) & (lane < e_lo + EXPERTS_PER_GROUP)
        le = jnp.where(in_grp, logits, neg)
        m1 = jnp.max(le, axis=-1, keepdims=True)
        i1 = jnp.min(jnp.where(le == m1, lane, big), axis=-1, keepdims=True)
        le2 = jnp.where(lane == i1, neg, le)
        m2 = jnp.max(le2, axis=-1, keepdims=True)
        i2 = jnp.min(jnp.where(le2 == m2, lane, big), axis=-1, keepdims=True)
        e2 = jnp.exp(m2 - m1)
        w1 = g_w / (1.0 + e2)
        w2 = g_w * e2 / (1.0 + e2)
        h2g_ref[rows, d:] = jnp.where(lane == i1, w1, 0.0) + jnp.where(lane == i2, w2, 0.0)

        ka = jnp.minimum(i1, i2) - e_lo
        kb = jnp.maximum(i1, i2) - e_lo
        pair = ka * (5.0 - ka) * 0.5 + kb - 1.0
        return g_idx * float(PAIRS_PER_GROUP) + pair

    bucket = jnp.concatenate(each(route, chunks, h2), axis=0)

    @pl.when(pl.program_id(0) == 0)
    def _():
        carry_ref[...] = jnp.zeros_like(carry_ref)

    blk = RANK_BLOCK
    earlier = (lax.broadcasted_iota(jnp.int32, (blk, blk), 1)
               < lax.broadcasted_iota(jnp.int32, (blk, blk), 0)).astype(BF16)
    carry = carry_ref[...]
    for r0 in range(0, tp, blk):
        bkt = bucket[r0:r0 + blk]
        oh = jnp.where(lane == bkt, 1.0, 0.0)
        prefix = jnp.dot(earlier, oh.astype(BF16), preferred_element_type=F32)
        rank = jnp.sum(jnp.where(lane == bkt, prefix + carry, 0.0), axis=-1, keepdims=True)
        dest_ref[r0:r0 + blk, :] = (bkt * float(n_tokens) + rank).astype(jnp.int32)
        carry = carry + jnp.sum(oh, axis=0, keepdims=True)
    carry_ref[...] = carry
    cnt_ref[0] = carry


def _scatter_rows_kernel(dest_ref, fill_start_ref, fill_n_ref, x_ref, o_hbm, zrow_ref, sem, zsem):
    i = pl.program_id(0)
    tg = x_ref.shape[0]
    base = i * tg

    @pl.loop(0, tg // SUBLANES)
    def _(r8):
        r0 = pl.multiple_of(r8 * SUBLANES, SUBLANES)
        for k in range(SUBLANES):
            pltpu.make_async_copy(x_ref.at[pl.ds(r0 + k, 1)],
                                  o_hbm.at[pl.ds(dest_ref[base + r0 + k], 1)], sem).start()

    pltpu.make_async_copy(x_ref, o_hbm.at[pl.ds(0, tg)], sem).wait()

    @pl.when(i == pl.num_programs(0) - 1)
    def _():
        zrow_ref[...] = jnp.zeros_like(zrow_ref)

        def fill(b, act):
            n = fill_n_ref[b]
            pos = fill_start_ref[b]
            head = jnp.minimum((-pos) & (SUBLANES - 1), n)
            zcopy = lambda at, size: pltpu.make_async_copy(zrow_ref.at[pl.ds(0, size)],
                                                           o_hbm.at[pl.ds(at, size)], zsem)
            for k in range(SUBLANES - 1):
                pl.when(k < head)(functools.partial(act, zcopy(pos + k, 1)))
            pos = pos + head
            n = n - head
            size = zrow_ref.shape[0]
            while size >= SUBLANES:
                pl.when((n & size) != 0)(functools.partial(act, zcopy(pl.multiple_of(pos, SUBLANES), size)))
                pos = pos + (n & size)
                size //= 2

        pl.loop(0, fill_n_ref.shape[0])(lambda b: fill(b, lambda c: c.start()))
        pl.loop(0, fill_n_ref.shape[0])(lambda b: fill(b, lambda c: c.wait()))


def _moe_pair_kernel(blk_ref, ea_ref, eb_ref, nvalid_ref, x_ref, gmoe_ref,
                     wga_ref, wua_ref, wda_ref, wgb_ref, wub_ref, wdb_ref, o_ref):
    i = pl.program_id(0)
    d = o_ref.shape[-1]

    @pl.when(i < nvalid_ref[0])
    def _():
        x = x_ref[...]
        h2 = x[:, 0:d]
        gates = x[:, d:]
        t = _rms(h2, gmoe_ref[...]).astype(BF16)
        lane = lax.broadcasted_iota(jnp.int32, (1, LANES), 1)

        def hidden(e, wg_ref, wu_ref):
            ge = jnp.sum(jnp.where(lane == N_GROUPS + e, gates, 0.0), axis=-1, keepdims=True)
            a = jnp.dot(t, wg_ref[0], preferred_element_type=F32)
            u = jnp.dot(t, wu_ref[0], preferred_element_type=F32)
            return (a * jax.nn.sigmoid(a) * u * ge).astype(BF16)

        ha = hidden(ea_ref[i], wga_ref, wua_ref)
        hb = hidden(eb_ref[i], wgb_ref, wub_ref)
        o_ref[...] = (h2 + jnp.dot(ha, wda_ref[0], preferred_element_type=F32)
                      + jnp.dot(hb, wdb_ref[0], preferred_element_type=F32))


def _gather_rows_kernel(tile_ref, sub_ref, y_hbm, o_ref, sem):
    n8 = o_ref.shape[0]
    base = pl.program_id(0) * n8 * SUBLANES

    @pl.loop(0, n8)
    def _(r8):
        r0 = base + r8 * SUBLANES
        for k in range(SUBLANES):
            pltpu.make_async_copy(y_hbm.at[tile_ref[r0 + k], pl.ds(sub_ref[r0 + k] & (SUBLANES - 1), 1)],
                                  o_ref.at[r8, pl.ds(k, 1)], sem).start()

    pltpu.make_async_copy(y_hbm.at[pl.ds(0, n8)], o_ref, sem).wait()


def _rope_block(x1, x2):
    z = jnp.zeros(x1.shape[:-1] + (32,), x1.dtype)
    return jnp.concatenate([x1, x1, z, x2, x2, z], axis=-1)


def _pair_rope_block(a, b):
    z = jnp.zeros(a.shape[:-1] + (32,), a.dtype)
    return jnp.concatenate([a[..., :16], b[..., :16], z, a[..., 16:], b[..., 16:], z], axis=-1)


def _row(v):
    return v.reshape(1, -1).astype(F32)


def kernel(x, mem, positions, norm_mix_g, w_in, mla_q_norm_g, w_uq, mla_kv_norm_g, w_ukv, mla_qn_g, mla_qr_g, mla_kn_g, mla_kr_g, sb_out_g, mla_out_g, w_o, norm_mem_g, mem_src_g, w_mq, w_mkv, mem_qhead_g, mem_khead_g, w_mo, norm_moe_g, w_group, b_group, w_router, b_router, w_gate, w_up, w_down):
    B, S, D = x.shape
    T = B * S
    M = mem.shape[1]
    FF = w_gate.shape[-1]
    half = MLA_ROPE // 2
    qk = MLA_NOPE + MLA_ROPE

    o = 3 * SB_WIDTH + MLA_Q_RANK + MLA_KV_RANK
    w_sb = jnp.concatenate([w_in[:, :SB_WIDTH] * (SB_HEAD_DIM ** -0.5 * LOG2E),
                            w_in[:, SB_WIDTH:3 * SB_WIDTH]], axis=1).astype(BF16)
    w_lat = jnp.concatenate([w_in[:, 3 * SB_WIDTH:o],
                             _rope_block(w_in[:, o:o + half], w_in[:, o + half:])], axis=1).astype(BF16)
    uq = w_uq.reshape(MLA_Q_RANK, MLA_HEADS, qk)
    ukv = w_ukv.reshape(MLA_KV_RANK, MLA_HEADS, MLA_NOPE + MLA_V)
    zk = jnp.zeros((MLA_KV_RANK, LANES), F32)
    uq_cols, uk_cols = [], []
    for p in range(N_PAIRS):
        a, b = 2 * p, 2 * p + 1
        uq_cols += [uq[:, a, :MLA_NOPE], uq[:, b, :MLA_NOPE], _pair_rope_block(uq[:, a, MLA_NOPE:], uq[:, b, MLA_NOPE:])]
        uk_cols += [ukv[:, a, :MLA_NOPE], ukv[:, b, :MLA_NOPE], zk]
    wuq = jnp.concatenate(uq_cols, axis=1).astype(BF16)
    wukv = jnp.concatenate(uk_cols + [ukv[:, h, MLA_NOPE:] for h in range(MLA_HEADS)], axis=1).astype(BF16)
    gq_pair = jnp.concatenate([mla_qn_g, mla_qn_g, _pair_rope_block(mla_qr_g, mla_qr_g)])
    gk_pair = jnp.concatenate([mla_kn_g, mla_kn_g, jnp.zeros((LANES,), F32)])
    gqcat = _row(jnp.tile(gq_pair, N_PAIRS))
    gkcat = _row(jnp.tile(gk_pair, N_PAIRS))
    gkr = _row(_rope_block(mla_kr_g[:half], mla_kr_g[half:]))
    inv_freq = ROPE_THETA ** (-(jnp.arange(half, dtype=F32) * 2.0 / MLA_ROPE))
    invf = _row(_rope_block(inv_freq, inv_freq))
    sgn = _row(jnp.concatenate([-jnp.ones((64,), F32), jnp.ones((64,), F32)]))
    wr = jnp.concatenate([w_group, w_router, jnp.zeros((D, LANES - N_GROUPS - N_EXPERTS), F32)], axis=1)
    wr_hi = wr.astype(BF16)
    wr_cat = jnp.concatenate([wr_hi, (wr - wr_hi.astype(F32)).astype(BF16)], axis=1)
    br = _row(jnp.concatenate([b_group, b_router, jnp.zeros((LANES - N_GROUPS - N_EXPERTS,), F32)]))

    full = lambda shape: pl.BlockSpec(shape, lambda *_: (0,) * len(shape))

    mk, mv = pl.pallas_call(
        _mem_kv_kernel, grid=(B,), name="mem_kv",
        in_specs=[pl.BlockSpec((1, M, D), lambda b: (b, 0, 0)), full((1, D)), full((D, 2 * D)),
                  full((1, D // MEM_HEADS))],
        out_specs=[pl.BlockSpec((1, M, D), lambda b: (b, 0, 0))] * 2,
        out_shape=[jax.ShapeDtypeStruct((B, M, D), BF16)] * 2,
    )(mem, _row(mem_src_g), w_mkv.astype(BF16), _row(mem_khead_g))

    tm = TILE_IN_PROJ
    rows = lambda w: pl.BlockSpec((tm, w), lambda i: (i, 0))
    sbq, sbk, sbv, mq, mkk, mvv = pl.pallas_call(
        _in_proj_kernel, grid=(T // tm,), name="in_proj",
        in_specs=[rows(D), rows(1), full((1, D)), full(w_lat.shape), full(w_sb.shape),
                  full((1, MLA_Q_RANK)), full(wuq.shape),
                  full((1, MLA_KV_RANK)), full(wukv.shape), full(gqcat.shape), full(gkcat.shape),
                  full((1, LANES)), full((1, LANES)), full((1, LANES))],
        out_specs=[rows(SB_WIDTH)] * 3 + [rows(N_PAIRS * PAIR_W)] * 2
                  + [pl.BlockSpec((MLA_WIDTH, tm), lambda i: (0, i))],
        out_shape=[jax.ShapeDtypeStruct((T, SB_WIDTH), BF16)] * 3
                  + [jax.ShapeDtypeStruct((T, N_PAIRS * PAIR_W), BF16)] * 2
                  + [jax.ShapeDtypeStruct((MLA_WIDTH, T), BF16)],
        scratch_shapes=[pltpu.VMEM((tm, LANES), F32)] * 2,
    )(x.reshape(T, D), positions.reshape(T, 1), _row(norm_mix_g), w_lat, w_sb, _row(mla_q_norm_g), wuq,
      _row(mla_kv_norm_g), wukv, gqcat, gkcat, gkr, invf, sgn)

    qspec = lambda t, w: pl.BlockSpec((1, t, w), lambda b, p, i: (b, i, p))
    kvspec = lambda w: pl.BlockSpec((1, S, w), lambda b, p, i: (b, 0, p))
    tq = TILE_SB_Q
    const3 = lambda shape: pl.BlockSpec(shape, lambda b, p, i, *_: (0,) * len(shape))
    sb = pl.pallas_call(
        functools.partial(_sb_attn_kernel, tq=tq), grid=(B, SB_HEADS // 2, S // tq), name="sb_attn",
        in_specs=[qspec(tq, LANES), kvspec(LANES), kvspec(LANES)],
        out_specs=qspec(tq, LANES),
        out_shape=jax.ShapeDtypeStruct((B, S, SB_WIDTH), BF16),
    )(sbq.reshape(B, S, SB_WIDTH), sbk.reshape(B, S, SB_WIDTH), sbv.reshape(B, S, SB_WIDTH))
    tqm = TILE_MLA_Q
    gmax2 = lambda g: jnp.max(jnp.square(g))
    q_bound = jnp.sqrt(MLA_NOPE * gmax2(mla_qn_g) + MLA_ROPE * gmax2(mla_qr_g))
    k_bound = jnp.sqrt(MLA_NOPE * gmax2(mla_kn_g) + MLA_ROPE * gmax2(mla_kr_g))
    logit_bound = 1.02 * q_bound * k_bound * (qk ** -0.5) * LOG2E
    bounded = (logit_bound < MLA_SAFE_LOGIT).astype(jnp.int32).reshape(1)
    key_c = jnp.arange(tqm)[:, None] // CHUNK
    qry_c = (jnp.arange(2 * tqm)[None, :] % tqm) // CHUNK
    mla_diag = jnp.where(key_c <= qry_c, 0.0, -jnp.inf).astype(F32)
    mla = pl.pallas_call(
        functools.partial(_mla_attn_kernel, tq=tqm), name="mla_attn",
        grid_spec=pltpu.PrefetchScalarGridSpec(
            num_scalar_prefetch=1, grid=(B, N_PAIRS, S // tqm),
            in_specs=[pl.BlockSpec((1, tqm, PAIR_W), lambda b, p, i, *_: (b, i, p)),
                      pl.BlockSpec((1, S, PAIR_W), lambda b, p, i, *_: (b, 0, p)),
                      pl.BlockSpec((LANES, S), lambda b, p, i, *_: (p, b)),
                      const3((tqm, 2 * tqm))],
            out_specs=pl.BlockSpec((1, tqm, LANES), lambda b, p, i, *_: (b, i, p)),
            scratch_shapes=[pltpu.VMEM((1, 2 * tqm), F32), pltpu.VMEM((LANES, tqm), F32)]),
        out_shape=jax.ShapeDtypeStruct((B, S, MLA_WIDTH), BF16),
    )(bounded, mq.reshape(B, S, -1), mkk.reshape(B, S, -1), mvv, mla_diag)

    tp = TILE_POST_ATTN
    prow = lambda w: pl.BlockSpec((tp, w), lambda i: (i, 0))
    memspec = pl.BlockSpec((1, M, D), lambda i: ((i * tp) // S, 0, 0))
    W = D + LANES
    h2g, dest, cnt = pl.pallas_call(
        functools.partial(_post_attn_kernel, n_tokens=T), grid=(T // tp,), name="post_attn",
        in_specs=[prow(D), prow(SB_WIDTH), prow(MLA_WIDTH), full((1, SB_WIDTH)), full((1, MLA_WIDTH)),
                  full((D, D)), full((1, D)), full((D, D)), full((1, D // MEM_HEADS)), memspec, memspec,
                  full((D, D)), full((1, D)), full((D, 2 * LANES)), full((1, LANES))],
        out_specs=[prow(W), prow(1), pl.BlockSpec((1, 1, LANES), lambda i: (i, 0, 0))],
        out_shape=[jax.ShapeDtypeStruct((T, W), F32), jax.ShapeDtypeStruct((T, 1), jnp.int32),
                   jax.ShapeDtypeStruct((T // tp, 1, LANES), F32)],
        scratch_shapes=[pltpu.VMEM((1, LANES), F32)],
        compiler_params=pltpu.CompilerParams(dimension_semantics=("arbitrary",)),
    )(x.reshape(T, D), sb.reshape(T, SB_WIDTH), mla.reshape(T, MLA_WIDTH), _row(sb_out_g), _row(mla_out_g),
      w_o.astype(BF16), _row(norm_mem_g), w_mq.astype(BF16), _row(mem_qhead_g), mk, mv,
      w_mo.astype(BF16), _row(norm_moe_g), wr_cat, br)

    te = TILE_EXPERT
    tg = TILE_MOVE
    n_buckets = N_GROUPS * PAIRS_PER_GROUP
    n_steps = T // te + n_buckets
    counts = cnt[-1, 0, :n_buckets].astype(jnp.int32)
    tiles = (counts + te - 1) // te
    ends = jnp.cumsum(tiles)
    first_row = (ends - tiles) * te
    n_valid = ends[-1]
    slot = dest.reshape(T // LANES, LANES)
    slot_bucket = slot // T
    row0 = jnp.zeros_like(slot)
    for b in range(n_buckets):
        row0 = jnp.where(slot_bucket == b, first_row[b], row0)
    dest = (row0 + slot % T).reshape(T)
    step = jnp.minimum(jnp.arange(n_steps, dtype=jnp.int32), n_valid - 1)
    bucket = jnp.zeros_like(step)
    for b in range(n_buckets):
        bucket = bucket + (step >= ends[b]).astype(jnp.int32)
    pr = bucket % PAIRS_PER_GROUP
    pair_a = (pr >= 3).astype(jnp.int32) + (pr >= 5).astype(jnp.int32)
    pair_b = pr + 1 - 2 * (pr >= 3).astype(jnp.int32) - (pr >= 5).astype(jnp.int32)
    ea = bucket // PAIRS_PER_GROUP * EXPERTS_PER_GROUP + pair_a
    eb = bucket // PAIRS_PER_GROUP * EXPERTS_PER_GROUP + pair_b

    any_spec = pl.BlockSpec(memory_space=pl.ANY)
    h2s = pl.pallas_call(
        _scatter_rows_kernel, name="moe_dispatch",
        grid_spec=pltpu.PrefetchScalarGridSpec(
            num_scalar_prefetch=3, grid=(T // tg,),
            in_specs=[pl.BlockSpec((tg, W), lambda i, *_: (i, 0))],
            out_specs=any_spec,
            scratch_shapes=[pltpu.VMEM((te // 2, W), F32),
                            pltpu.SemaphoreType.DMA, pltpu.SemaphoreType.DMA]),
        out_shape=jax.ShapeDtypeStruct((n_steps * te, W), F32),
        compiler_params=pltpu.CompilerParams(dimension_semantics=("arbitrary",)),
    )(dest, first_row + counts, tiles * te - counts, h2g)

    wspec = lambda shape, which: pl.BlockSpec((1,) + shape, lambda i, blk, ea, eb, nv: ((ea, eb)[which][i], 0, 0))
    w_gate_b, w_up_b, w_down_b = w_gate.astype(BF16), w_up.astype(BF16), w_down.astype(BF16)
    ys = pl.pallas_call(
        _moe_pair_kernel, name="moe_experts",
        grid_spec=pltpu.PrefetchScalarGridSpec(
            num_scalar_prefetch=4, grid=(n_steps,),
            in_specs=[pl.BlockSpec((te, W), lambda i, blk, *_: (blk[i], 0)),
                      pl.BlockSpec((1, D), lambda i, *_: (0, 0)),
                      wspec((D, FF), 0), wspec((D, FF), 0), wspec((FF, D), 0),
                      wspec((D, FF), 1), wspec((D, FF), 1), wspec((FF, D), 1)],
            out_specs=pl.BlockSpec((te, D), lambda i, blk, *_: (blk[i], 0))),
        out_shape=jax.ShapeDtypeStruct((n_steps * te, D), F32),
        compiler_params=pltpu.CompilerParams(dimension_semantics=("arbitrary",)),
    )(step, ea, eb, n_valid.reshape(1), h2s, _row(norm_moe_g),
      w_gate_b, w_up_b, w_down_b, w_gate_b, w_up_b, w_down_b)

    out = pl.pallas_call(
        _gather_rows_kernel, name="moe_combine",
        grid_spec=pltpu.PrefetchScalarGridSpec(
            num_scalar_prefetch=2, grid=(T // tg,),
            in_specs=[any_spec],
            out_specs=pl.BlockSpec((tg // SUBLANES, SUBLANES, D), lambda i, *_: (i, 0, 0)),
            scratch_shapes=[pltpu.SemaphoreType.DMA]),
        out_shape=jax.ShapeDtypeStruct((T // SUBLANES, SUBLANES, D), F32),
        compiler_params=pltpu.CompilerParams(dimension_semantics=("arbitrary",)),
    )(dest // SUBLANES, dest % SUBLANES, ys.reshape(-1, SUBLANES, D))
    return out.reshape(B, S, D)
```
